```python
import jax
import jax.numpy as jnp
from jax import lax
import numpy as np

D_MODEL = 1024
BATCH = 4
SEQ = 8192
DEPTH = 4

GRID_W = 64
CTX_LEN = 256
N_MIXERS = 2
N_HEADS = 16
N_KV_HEADS = 4
HEAD_DIM = 64
GROUP = N_HEADS // N_KV_HEADS
Q_DIM = N_HEADS * HEAD_DIM
KV_DIM = N_KV_HEADS * HEAD_DIM
WINDOW = 128
BLOCK = 128
ROPE_BASE = 10000.0
ROPE_HALF = HEAD_DIM // 4
POOL_WINDOWS = (2, 4, 8, 16)
N_POOL_GROUPS = len(POOL_WINDOWS)
POOL_GROUP_DIM = D_MODEL // N_POOL_GROUPS
D_FF = 2816
N_EXPERTS = 8
TOP_K = 2
D_FF_EXPERT = 3584
NORM_EPS = 1e-6
NEG_INF = -1e30
N_ATTN_LAYERS = (DEPTH + 1) // 2
N_POOL_LAYERS = DEPTH // 2

kernel_name = 'hybrid_swa_pool_moe_dit'


def rmsnorm(x, g):
    xf = x.astype(jnp.float32)
    y = xf * lax.rsqrt(jnp.mean(xf * xf, axis=-1, keepdims=True) + NORM_EPS)
    return (y * g.astype(jnp.float32)).astype(x.dtype)


def ada_mod(cond, w, b):
    m = jax.nn.silu(cond) @ w + b
    return jnp.split(m[..., None, :], 6, axis=-1)


def modulate(h, shift, scale):
    return h * (1 + scale) + shift


def axial_rope_tables(n_tokens):
    rows = n_tokens // GRID_W
    row = jnp.broadcast_to(jnp.arange(rows)[:, None], (rows, GRID_W)).reshape(-1).astype(jnp.float32)
    col = jnp.broadcast_to(jnp.arange(GRID_W)[None, :], (rows, GRID_W)).reshape(-1).astype(jnp.float32)
    inv_freq = ROPE_BASE ** (-jnp.arange(ROPE_HALF, dtype=jnp.float32) / ROPE_HALF)
    ang_r = row[:, None] * inv_freq
    ang_c = col[:, None] * inv_freq
    return (jnp.cos(ang_r), jnp.sin(ang_r), jnp.cos(ang_c), jnp.sin(ang_c))


def _rotate(v, cos, sin):
    v1, v2 = v[..., :ROPE_HALF], v[..., ROPE_HALF:]
    return jnp.concatenate([v1 * cos - v2 * sin, v2 * cos + v1 * sin], axis=-1)


def apply_axial_rope(t, tables):
    bshape = (t.shape[1],) + (1,) * (t.ndim - 3) + (ROPE_HALF,)
    cos_r, sin_r, cos_c, sin_c = [a.reshape(bshape) for a in tables]
    tf = t.astype(jnp.float32)
    half = HEAD_DIM // 2
    out = jnp.concatenate([_rotate(tf[..., :half], cos_r, sin_r),
                           _rotate(tf[..., half:], cos_c, sin_c)], axis=-1)
    return out.astype(t.dtype)


def q_proj(h, w_qkv):
    B, T = h.shape[:2]
    return (h @ w_qkv[:, :Q_DIM]).reshape(B, T, N_KV_HEADS, GROUP, HEAD_DIM)


def kv_proj(h, w_qkv):
    B, T = h.shape[:2]
    k, v = jnp.split(h @ w_qkv[:, Q_DIM:], 2, axis=-1)
    return k.reshape(B, T, N_KV_HEADS, HEAD_DIM), v.reshape(B, T, N_KV_HEADS, HEAD_DIM)


def sink_softmax(scores, sink):
    m = sink
    for s in scores:
        m = jnp.maximum(m, jnp.max(s, axis=-1, keepdims=True))
    probs = [jnp.exp(s - m) for s in scores]
    denom = jnp.exp(sink - m)
    for p in probs:
        denom = denom + jnp.sum(p, axis=-1, keepdims=True)
    return [p / denom for p in probs]


def latent_window_attention(q, k, v, k_ctx, v_ctx, sink):
    B, L = q.shape[:2]
    nb = L // BLOCK
    scale = HEAD_DIM ** -0.5
    qb = q.reshape(B, nb, BLOCK, N_KV_HEADS, GROUP, HEAD_DIM)

    def band(t):
        tp = jnp.pad(t, ((0, 0), (BLOCK, BLOCK), (0, 0), (0, 0))).reshape(B, nb + 2, BLOCK, N_KV_HEADS, HEAD_DIM)
        return jnp.concatenate([tp[:, :-2], tp[:, 1:-1], tp[:, 2:]], axis=2)

    kb, vb = band(k), band(v)
    s_loc = jnp.einsum('bnqhgd,bnkhd->bnhgqk', qb, kb, preferred_element_type=jnp.float32) * scale
    q_pos = jnp.arange(nb)[:, None] * BLOCK + jnp.arange(BLOCK)[None, :]
    k_pos = (jnp.arange(nb)[:, None] - 1) * BLOCK + jnp.arange(3 * BLOCK)[None, :]
    rel = k_pos[:, None, :] - q_pos[:, :, None]
    valid = (jnp.abs(rel) <= WINDOW) & (k_pos[:, None, :] >= 0) & (k_pos[:, None, :] < L)
    s_loc = jnp.where(valid[None, :, None, None, :, :], s_loc, NEG_INF)
    s_ctx = jnp.einsum('bnqhgd,bchd->bnhgqc', qb, k_ctx, preferred_element_type=jnp.float32) * scale
    p_loc, p_ctx = sink_softmax([s_loc, s_ctx], sink)
    o = (jnp.einsum('bnhgqk,bnkhd->bnqhgd', p_loc.astype(vb.dtype), vb)
         + jnp.einsum('bnhgqc,bchd->bnqhgd', p_ctx.astype(v_ctx.dtype), v_ctx))
    return o.reshape(B, L, Q_DIM)


def context_attention(q, k, v, sink):
    B, C = q.shape[:2]
    s = jnp.einsum('bqhgd,bkhd->bhgqk', q, k, preferred_element_type=jnp.float32) * HEAD_DIM ** -0.5
    (p,) = sink_softmax([s], sink)
    o = jnp.einsum('bhgqk,bkhd->bqhgd', p.astype(v.dtype), v)
    return o.reshape(B, C, Q_DIM)


def window_mean(h, w):
    T = h.shape[1]
    cs = jnp.pad(jnp.cumsum(h.astype(jnp.float32), axis=1), ((0, 0), (1, 0), (0, 0)))
    t = jnp.arange(T)
    lo = jnp.clip(t - w // 2, 0, T)
    hi = jnp.clip(t - w // 2 + w, 0, T)
    total = jnp.take(cs, hi, axis=1) - jnp.take(cs, lo, axis=1)
    return total / (hi - lo).astype(jnp.float32)[None, :, None]


def pool_mixer(h, pool_w, pool_b, pool_scale):
    B, T, _ = h.shape
    groups = jnp.split(h, N_POOL_GROUPS, axis=-1)
    d = jnp.stack([window_mean(g, w) - g.astype(jnp.float32) for g, w in zip(groups, POOL_WINDOWS)], axis=2)
    y = jnp.einsum('btgc,gcd->btgd', d.astype(h.dtype), pool_w) + pool_b.reshape(N_POOL_GROUPS, POOL_GROUP_DIM)
    return y.reshape(B, T, D_MODEL) * pool_scale


def swiglu(h, w_gu, w_down):
    g, u = jnp.split(h @ w_gu, 2, axis=-1)
    return (jax.nn.silu(g) * u) @ w_down


def moe_swiglu(h, router_w, router_b, w_gu, w_down):
    logits = (h @ router_w).astype(jnp.float32) + router_b.astype(jnp.float32)
    top_v, top_i = lax.top_k(logits, TOP_K)
    gates = jax.nn.softmax(top_v, axis=-1)
    combine = jnp.einsum('btk,btke->bte', gates, jax.nn.one_hot(top_i, N_EXPERTS, dtype=jnp.float32))
    y = jnp.zeros(h.shape, jnp.float32)
    for e in range(N_EXPERTS):
        y = y + combine[..., e:e + 1] * swiglu(h, w_gu[e], w_down[e]).astype(jnp.float32)
    return y.astype(h.dtype)


def setup_inputs(seed: int = 0) -> dict:
    key = jax.random.key(seed)
    ks = jax.random.split(key, 20)
    f32 = jnp.float32

    def dense(k, shape, fan_in):
        return jax.random.normal(k, shape, f32) * fan_in ** -0.5

    return {
        'x': jax.random.normal(ks[0], (BATCH, SEQ, D_MODEL), f32),
        'c': jax.random.normal(ks[1], (BATCH, D_MODEL), f32),
        'ctx': jax.random.normal(ks[2], (BATCH, CTX_LEN, D_MODEL), f32),
        'c_ctx': jax.random.normal(ks[3], (D_MODEL,), f32),
        'ada_w': 0.5 * dense(ks[4], (DEPTH, D_MODEL, 6 * D_MODEL), D_MODEL),
        'ada_b': 0.02 * jax.random.normal(ks[5], (DEPTH, 6 * D_MODEL), f32),
        'norm_g': 1.0 + 0.1 * jax.random.normal(ks[6], (DEPTH, 4, D_MODEL), f32),
        'attn_w_qkv': dense(ks[7], (N_ATTN_LAYERS, D_MODEL, Q_DIM + 2 * KV_DIM), D_MODEL),
        'attn_w_o': dense(ks[8], (N_ATTN_LAYERS, Q_DIM, D_MODEL), Q_DIM),
        'attn_sink': jax.random.normal(ks[9], (N_ATTN_LAYERS, N_HEADS), f32),
        'pool_w': dense(ks[10], (N_POOL_LAYERS, N_POOL_GROUPS, POOL_GROUP_DIM, POOL_GROUP_DIM), POOL_GROUP_DIM),
        'pool_b': 0.02 * jax.random.normal(ks[11], (N_POOL_LAYERS, D_MODEL), f32),
        'pool_scale': 1.0 + 0.1 * jax.random.normal(ks[12], (N_POOL_LAYERS, D_MODEL), f32),
        'ffn_w_gu': dense(ks[13], (N_ATTN_LAYERS, D_MODEL, 2 * D_FF), D_MODEL),
        'ffn_w_down': dense(ks[14], (N_ATTN_LAYERS, D_FF, D_MODEL), D_FF),
        'router_w': dense(ks[15], (N_POOL_LAYERS, D_MODEL, N_EXPERTS), D_MODEL),
        'router_b': 0.01 * jax.random.normal(ks[16], (N_POOL_LAYERS, N_EXPERTS), f32),
        'moe_w_gu': dense(ks[17], (N_POOL_LAYERS, N_EXPERTS, D_MODEL, 2 * D_FF_EXPERT), D_MODEL),
        'moe_w_down': dense(ks[18], (N_POOL_LAYERS, N_EXPERTS, D_FF_EXPERT, D_MODEL), D_FF_EXPERT),
    }


def reference(x, c, ctx, c_ctx, ada_w, ada_b, norm_g, attn_w_qkv, attn_w_o, attn_sink,
              pool_w, pool_b, pool_scale, ffn_w_gu, ffn_w_down, router_w, router_b,
              moe_w_gu, moe_w_down):
    rope = axial_rope_tables(x.shape[1])
    is_attn = [i % N_MIXERS == 0 for i in range(DEPTH)]
    for i in range(DEPTH):
        slot = i // N_MIXERS
        ctx_needed_later = any(is_attn[i + 1:])
        ctx_needed_now = is_attn[i] or ctx_needed_later
        sh_m, sc_m, gt_m, sh_f, sc_f, gt_f = ada_mod(c, ada_w[i], ada_b[i])
        g_pre_mix, g_post_mix, g_pre_ffn, g_post_ffn = norm_g[i]

        h = modulate(rmsnorm(x, g_pre_mix), sh_m, sc_m)
        if ctx_needed_now:
            csh_m, csc_m, cgt_m, csh_f, csc_f, cgt_f = ada_mod(c_ctx, ada_w[i], ada_b[i])
            hc = modulate(rmsnorm(ctx, g_pre_mix), csh_m, csc_m)
        if is_attn[i]:
            w_qkv = attn_w_qkv[slot]
            sink = attn_sink[slot].astype(jnp.float32).reshape(N_KV_HEADS, GROUP, 1, 1)
            q = apply_axial_rope(q_proj(h, w_qkv), rope)
            k, v = kv_proj(h, w_qkv)
            k = apply_axial_rope(k, rope)
            k_c, v_c = kv_proj(hc, w_qkv)
            y = latent_window_attention(q, k, v, k_c, v_c, sink) @ attn_w_o[slot]
            if ctx_needed_later:
                yc = context_attention(q_proj(hc, w_qkv), k_c, v_c, sink) @ attn_w_o[slot]
        else:
            y = pool_mixer(h, pool_w[slot], pool_b[slot], pool_scale[slot])
            if ctx_needed_later:
                yc = pool_mixer(hc, pool_w[slot], pool_b[slot], pool_scale[slot])
        x = x + gt_m * rmsnorm(y, g_post_mix)

        h = modulate(rmsnorm(x, g_pre_ffn), sh_f, sc_f)
        if i % 2 == 0:
            f = swiglu(h, ffn_w_gu[slot], ffn_w_down[slot])
        else:
            f = moe_swiglu(h, router_w[slot], router_b[slot], moe_w_gu[slot], moe_w_down[slot])
        x = x + gt_f * rmsnorm(f, g_post_ffn)

        if ctx_needed_later:
            ctx = ctx + cgt_m * rmsnorm(yc, g_post_mix)
            hc = modulate(rmsnorm(ctx, g_pre_ffn), csh_f, csc_f)
            if i % 2 == 0:
                fc = swiglu(hc, ffn_w_gu[slot], ffn_w_down[slot])
            else:
                fc = moe_swiglu(hc, router_w[slot], router_b[slot], moe_w_gu[slot], moe_w_down[slot])
            ctx = ctx + cgt_f * rmsnorm(fc, g_post_ffn)
    return x
```

```python
import functools

import jax
import jax.numpy as jnp
from jax import lax
from jax.experimental import pallas as pl
from jax.experimental.pallas import tpu as pltpu

F32 = jnp.float32
BF16 = jnp.bfloat16

N_HEADS = 16
N_KV_HEADS = 4
HEAD_DIM = 64
GROUP = N_HEADS // N_KV_HEADS
ATTN_BLOCK = 128
GRID_W = 64
ROPE_BASE = 10000.0
ROPE_HALF = HEAD_DIM // 4
POOL_WINDOWS = (2, 4, 8, 16)
POOL_HALO = 8
N_EXPERTS = 8
TOP_K = 2
NORM_EPS = 1e-6
NEG_INF = -1e30
LANES = 128
MOD_ROWS = 8
VMEM_LIMIT = 56 * 1024 * 1024


def _cparams(n_axes):
    return pltpu.CompilerParams(dimension_semantics=("arbitrary",) * n_axes,
                                vmem_limit_bytes=VMEM_LIMIT)


def _rmsnorm(x, g):
    return x * lax.rsqrt(jnp.mean(x * x, axis=-1, keepdims=True) + NORM_EPS) * g


def _norm_mod(x, g, shift, scale):
    return _rmsnorm(x, g) * (1.0 + scale) + shift


def _silu(x):
    return x / (1.0 + jnp.exp(-x))


def _dot(a, b):
    return jnp.dot(a, b, preferred_element_type=F32)


def _ada_kernel(cond_ref, w_ref, b_ref, o_ref):
    s = _silu(cond_ref[...]).astype(BF16)
    o_ref[0] = _dot(s, w_ref[0].astype(BF16)) + b_ref[0]


def _ada_mods(cond, ada_w, ada_b):
    depth, d, n = ada_w.shape
    tn = 1536
    return pl.pallas_call(
        _ada_kernel,
        grid=(depth, n // tn),
        in_specs=[
            pl.BlockSpec((MOD_ROWS, d), lambda l, j: (0, 0)),
            pl.BlockSpec((1, d, tn), lambda l, j: (l, 0, j)),
            pl.BlockSpec((1, 1, tn), lambda l, j: (l, 0, j)),
        ],
        out_specs=pl.BlockSpec((1, MOD_ROWS, tn), lambda l, j: (l, 0, j)),
        out_shape=jax.ShapeDtypeStruct((depth, MOD_ROWS, n), F32),
        compiler_params=_cparams(2),
        name="ada_mods",
    )(cond, ada_w, ada_b.reshape(depth, 1, n))


def _rope(chunk, cos, sin, take_upper):
    rot = jnp.where(take_upper, pltpu.roll(chunk, LANES - ROPE_HALF, axis=1),
                    pltpu.roll(chunk, ROPE_HALF, axis=1))
    return chunk * cos + rot * sin


def _qkv_kernel(*refs, rope):
    if rope:
        x_ref, g_ref, mod_ref, w_ref, cos_ref, sin_ref, q_ref, kv_ref = refs
    else:
        x_ref, g_ref, mod_ref, w_ref, q_ref, kv_ref = refs
    h = _norm_mod(x_ref[0], g_ref[0:1, :], mod_ref[0, 0:1, :], mod_ref[0, 1:2, :])
    y = _dot(h.astype(BF16), w_ref[...])
    q_dim = N_HEADS * HEAD_DIM
    kv_dim = N_KV_HEADS * HEAD_DIM
    if rope:
        cos, sin = cos_ref[...], sin_ref[...]
        lane = lax.broadcasted_iota(jnp.int32, cos.shape, 1)
        take_upper = (lane % (2 * ROPE_HALF)) < ROPE_HALF
    for c in range(q_dim // LANES):
        chunk = y[:, c * LANES:(c + 1) * LANES]
        if rope:
            chunk = _rope(chunk, cos, sin, take_upper)
        q_ref[0, :, c * LANES:(c + 1) * LANES] = (chunk * HEAD_DIM ** -0.5).astype(BF16)
    for c in range(kv_dim // LANES):
        k = y[:, q_dim + c * LANES:q_dim + (c + 1) * LANES]
        if rope:
            k = _rope(k, cos, sin, take_upper)
        v = y[:, q_dim + kv_dim + c * LANES:q_dim + kv_dim + (c + 1) * LANES]
        kv_ref[0, :, c * LANES:(c + 1) * LANES] = k.astype(BF16)
        kv_ref[0, :, kv_dim + c * LANES:kv_dim + (c + 1) * LANES] = pltpu.roll(k, HEAD_DIM, axis=1).astype(BF16)
        kv_ref[0, :, 2 * kv_dim + c * LANES:2 * kv_dim + (c + 1) * LANES] = v.astype(BF16)
        kv_ref[0, :, 3 * kv_dim + c * LANES:3 * kv_dim + (c + 1) * LANES] = pltpu.roll(v, HEAD_DIM, axis=1).astype(BF16)


def _qkv_proj(x, norm_g, mods, mod_row, w_qkv, rope_tables):
    b, l, d = x.shape
    tm = min(512, l)
    n = w_qkv.shape[1]
    rope = rope_tables is not None
    in_specs = [
        pl.BlockSpec((1, tm, d), lambda bi, i: (bi, i, 0)),
        pl.BlockSpec(norm_g.shape, lambda bi, i: (0, 0)),
        pl.BlockSpec((1, 6, d), lambda bi, i: (mod_row(bi), 0, 0)),
        pl.BlockSpec((d, n), lambda bi, i: (0, 0)),
    ]
    args = [x, norm_g, mods, w_qkv]
    if rope:
        in_specs += [pl.BlockSpec((tm, LANES), lambda bi, i: (i, 0))] * 2
        args += list(rope_tables)
    q_dim = N_HEADS * HEAD_DIM
    return pl.pallas_call(
        functools.partial(_qkv_kernel, rope=rope),
        grid=(b, l // tm),
        in_specs=in_specs,
        out_specs=[pl.BlockSpec((1, tm, q_dim), lambda bi, i: (bi, i, 0)),
                   pl.BlockSpec((1, tm, q_dim), lambda bi, i: (bi, i, 0))],
        out_shape=[jax.ShapeDtypeStruct((b, l, q_dim), BF16),
                   jax.ShapeDtypeStruct((b, l, q_dim), BF16)],
        compiler_params=_cparams(2),
        name="qkv_rope" if rope else "qkv",
    )(*args)


def _rope_tables(n_tokens):
    t = jnp.arange(n_tokens)
    row = (t // GRID_W).astype(F32)
    col = (t % GRID_W).astype(F32)
    inv_freq = ROPE_BASE ** (-jnp.arange(ROPE_HALF, dtype=F32) / ROPE_HALF)
    ang_r = row[:, None] * inv_freq
    ang_c = col[:, None] * inv_freq
    cos_h = jnp.concatenate([jnp.cos(ang_r), jnp.cos(ang_r), jnp.cos(ang_c), jnp.cos(ang_c)], axis=-1)
    sin_h = jnp.concatenate([-jnp.sin(ang_r), jnp.sin(ang_r), -jnp.sin(ang_c), jnp.sin(ang_c)], axis=-1)
    reps = LANES // HEAD_DIM
    return jnp.tile(cos_h, (1, reps)), jnp.tile(sin_h, (1, reps))


def _pad_heads(dst_ref, rows, src):
    kv_dim = N_KV_HEADS * HEAD_DIM
    n = src.shape[0]
    lane = lax.broadcasted_iota(jnp.int32, (n, LANES), 1).astype(F32).astype(BF16)
    low = lane < HEAD_DIM
    zero = jnp.zeros((n, LANES), BF16)
    for t in range(2):
        for h in range(N_KV_HEADS):
            c = h // 2
            plain = src[:, 2 * t * kv_dim + c * LANES:2 * t * kv_dim + (c + 1) * LANES]
            swapped = src[:, (2 * t + 1) * kv_dim + c * LANES:(2 * t + 1) * kv_dim + (c + 1) * LANES]
            in_low, in_high = (plain, swapped) if h % 2 == 0 else (swapped, plain)
            base = t * 2 * N_KV_HEADS * LANES + h * 2 * LANES
            dst_ref[rows, base:base + LANES] = jnp.where(low, in_low, zero)
            dst_ref[rows, base + LANES:base + 2 * LANES] = jnp.where(low, zero, in_high)


def _attn_kernel(*refs, tq, n_local_blocks, has_local):
    if has_local:
        sink_ref, q_ref, kvo_ref, kvp_ref, kvn_ref, kvc_ref, o_ref, loc_pad, ctx_pad = refs
    else:
        sink_ref, q_ref, kvc_ref, o_ref, ctx_pad = refs
    blk = ATTN_BLOCK
    i = pl.program_id(1)
    n_ctx = kvc_ref.shape[1]
    v_base = 2 * N_KV_HEADS * LANES
    _pad_heads(ctx_pad, slice(0, n_ctx), kvc_ref[0])
    if has_local:
        _pad_heads(loc_pad, slice(0, blk), kvp_ref[0])
        _pad_heads(loc_pad, slice(blk, blk + tq), kvo_ref[0])
        _pad_heads(loc_pad, slice(blk + tq, 2 * blk + tq), kvn_ref[0])

    lane_low = lax.broadcasted_iota(jnp.int32, (2 * blk, LANES), 1) < HEAD_DIM
    row_first = lax.broadcasted_iota(jnp.int32, (2 * blk, 1), 0) < blk

    def q_block(j, carry):
        r0 = pl.multiple_of(j * blk, blk)
        if has_local:
            n = i * (tq // blk) + j
            qi = lax.broadcasted_iota(jnp.int32, (2 * blk, 3 * blk), 0) & (blk - 1)
            ki = lax.broadcasted_iota(jnp.int32, (2 * blk, 3 * blk), 1)
            rel = ki - blk - qi
            k_lo = jnp.where(n == 0, blk, 0)
            k_hi = jnp.where(n == n_local_blocks - 1, 2 * blk, 3 * blk)
            valid = (rel >= -blk) & (rel <= blk) & (ki >= k_lo) & (ki < k_hi)
        for h in range(N_KV_HEADS):
            qp = q_ref[0, pl.ds(r0, blk), h * 2 * LANES:(h + 1) * 2 * LANES]
            lhs = jnp.concatenate([qp[:, :LANES], qp[:, LANES:]], axis=0)
            kc = h * 2 * LANES
            vc = v_base + h * 2 * LANES
            nt = (((1,), (1,)), ((), ()))
            seg_scores = [[], []]
            k_ctx = jnp.concatenate([ctx_pad[:, kc:kc + LANES], ctx_pad[:, kc + LANES:kc + 2 * LANES]], axis=0)
            s_ctx = lax.dot_general(lhs, k_ctx, nt, preferred_element_type=F32)
            seg_scores[0].append(s_ctx[:, :n_ctx])
            seg_scores[1].append(s_ctx[:, n_ctx:])
            if has_local:
                k_loc = jnp.concatenate([loc_pad[pl.ds(r0, 3 * blk), kc:kc + LANES],
                                         loc_pad[pl.ds(r0, 3 * blk), kc + LANES:kc + 2 * LANES]], axis=0)
                s_loc = lax.dot_general(lhs, k_loc, nt, preferred_element_type=F32)
                seg_scores[0].append(jnp.where(valid, s_loc[:, :3 * blk], NEG_INF))
                seg_scores[1].append(jnp.where(valid, s_loc[:, 3 * blk:], NEG_INF))
            probs, inv = [], []
            for seg in range(2):
                sink = jnp.where(row_first, sink_ref[h * GROUP + seg], sink_ref[h * GROUP + 2 + seg])
                m = sink
                for s in seg_scores[seg]:
                    m = jnp.maximum(m, jnp.max(s, axis=-1, keepdims=True))
                ps = [jnp.exp(s - m) for s in seg_scores[seg]]
                denom = jnp.exp(sink - m)
                for p in ps:
                    denom = denom + jnp.sum(p, axis=-1, keepdims=True)
                probs.append(ps)
                inv.append(1.0 / denom)
            p_ctx = jnp.concatenate([probs[0][0], probs[1][0]], axis=1).astype(BF16)
            v_ctx = jnp.concatenate([ctx_pad[:, vc:vc + LANES], ctx_pad[:, vc + LANES:vc + 2 * LANES]], axis=0)
            o = _dot(p_ctx, v_ctx)
            if has_local:
                p_loc = jnp.concatenate([probs[0][1], probs[1][1]], axis=1).astype(BF16)
                v_loc = jnp.concatenate([loc_pad[pl.ds(r0, 3 * blk), vc:vc + LANES],
                                         loc_pad[pl.ds(r0, 3 * blk), vc + LANES:vc + 2 * LANES]], axis=0)
                o = o + _dot(p_loc, v_loc)
            o = o * jnp.where(lane_low, inv[0], inv[1])
            o_ref[0, pl.ds(r0, blk), h * 2 * LANES:h * 2 * LANES + LANES] = o[:blk].astype(BF16)
            o_ref[0, pl.ds(r0, blk), h * 2 * LANES + LANES:(h + 1) * 2 * LANES] = o[blk:].astype(BF16)
        return carry

    lax.fori_loop(0, tq // blk, q_block, 0)


def _attention(q, kv, kv_ctx, sink, has_local):
    b, l, q_dim = q.shape
    n_ctx = kv_ctx.shape[1]
    blk = ATTN_BLOCK
    tq = min(1024, l)
    bpt = tq // blk
    n_blocks = l // blk
    pad_lanes = 4 * N_KV_HEADS * LANES
    in_specs = [pl.BlockSpec(memory_space=pltpu.SMEM),
                pl.BlockSpec((1, tq, q_dim), lambda bi, i: (bi, i, 0))]
    args = [sink, q]
    scratch = []
    if has_local:
        in_specs += [
            pl.BlockSpec((1, tq, q_dim), lambda bi, i: (bi, i, 0)),
            pl.BlockSpec((1, blk, q_dim), lambda bi, i: (bi, jnp.maximum(i * bpt - 1, 0), 0)),
            pl.BlockSpec((1, blk, q_dim), lambda bi, i: (bi, jnp.minimum((i + 1) * bpt, n_blocks - 1), 0)),
        ]
        args += [kv, kv, kv]
        scratch.append(pltpu.VMEM((tq + 2 * blk, pad_lanes), BF16))
    in_specs.append(pl.BlockSpec((1, n_ctx, q_dim), lambda bi, i: (bi, 0, 0)))
    args.append(kv_ctx)
    scratch.append(pltpu.VMEM((n_ctx, pad_lanes), BF16))
    return pl.pallas_call(
        functools.partial(_attn_kernel, tq=tq, n_local_blocks=n_blocks, has_local=has_local),
        grid=(b, l // tq),
        in_specs=in_specs,
        out_specs=pl.BlockSpec((1, tq, q_dim), lambda bi, i: (bi, i, 0)),
        out_shape=jax.ShapeDtypeStruct((b, l, q_dim), BF16),
        scratch_shapes=scratch,
        compiler_params=_cparams(2),
        name="window_attention" if has_local else "context_attention",
    )(*args)


def _oproj_ffn_kernel(a_ref, x_ref, g_ref, mod_ref, wo_ref, wgu_ref, wd_ref, o_ref):
    d_ff = wd_ref.shape[0]
    y = _dot(a_ref[0], wo_ref[...])
    x = x_ref[0] + mod_ref[0, 2:3, :] * _rmsnorm(y, g_ref[1:2, :])
    h = _norm_mod(x, g_ref[2:3, :], mod_ref[0, 3:4, :], mod_ref[0, 4:5, :]).astype(BF16)
    gate = _dot(h, wgu_ref[:, :d_ff])
    up = _dot(h, wgu_ref[:, d_ff:])
    f = _dot((_silu(gate) * up).astype(BF16), wd_ref[...])
    o_ref[0] = x + mod_ref[0, 5:6, :] * _rmsnorm(f, g_ref[3:4, :])


def _oproj_ffn(attn, x, norm_g, mods, mod_row, w_o, w_gu, w_down):
    b, l, d = x.shape
    tm = min(512, l)
    resident = pl.Buffered(1)
    return pl.pallas_call(
        _oproj_ffn_kernel,
        grid=(b, l // tm),
        in_specs=[
            pl.BlockSpec((1, tm, attn.shape[2]), lambda bi, i: (bi, i, 0)),
            pl.BlockSpec((1, tm, d), lambda bi, i: (bi, i, 0)),
            pl.BlockSpec(norm_g.shape, lambda bi, i: (0, 0)),
            pl.BlockSpec((1, 6, d), lambda bi, i: (mod_row(bi), 0, 0)),
            pl.BlockSpec(w_o.shape, lambda bi, i: (0, 0), pipeline_mode=resident),
            pl.BlockSpec(w_gu.shape, lambda bi, i: (0, 0), pipeline_mode=resident),
            pl.BlockSpec(w_down.shape, lambda bi, i: (0, 0), pipeline_mode=resident),
        ],
        out_specs=pl.BlockSpec((1, tm, d), lambda bi, i: (bi, i, 0)),
        out_shape=jax.ShapeDtypeStruct((b, l, d), F32),
        compiler_params=_cparams(2),
        name="oproj_ffn",
    )(attn, x, norm_g, mods, w_o, w_gu, w_down)


def _pool_kernel(x_ref, xp_ref, xn_ref, g_ref, mod_ref, pw_ref, pb_ref, ps_ref, rw_ref, rb_ref,
                 xo_ref, h2_ref, route_ref, hbuf, ybuf, *, tm, seq_len):
    i = pl.program_id(1)
    n_blk = pl.num_programs(1)
    sub = 128
    halo = POOL_HALO
    gd = pw_ref.shape[1]
    g_pre, shift, scale = g_ref[0:1, :], mod_ref[0, 0:1, :], mod_ref[0, 1:2, :]
    x = x_ref[0]
    hbuf[0:halo, :] = _norm_mod(xp_ref[0], g_pre, shift, scale) * jnp.where(i > 0, 1.0, 0.0)
    hbuf[halo:halo + tm, :] = _norm_mod(x, g_pre, shift, scale)
    hbuf[halo + tm:, :] = _norm_mod(xn_ref[0], g_pre, shift, scale) * jnp.where(i < n_blk - 1, 1.0, 0.0)

    r = lax.broadcasted_iota(jnp.int32, (sub, 2 * sub), 0)
    c = lax.broadcasted_iota(jnp.int32, (sub, 2 * sub), 1)
    t_row = lax.broadcasted_iota(jnp.int32, (sub, 1), 0)
    fill = jnp.zeros((sub - 2 * halo, gd), BF16)
    for gi, w in enumerate(POOL_WINDOWS):
        band = jnp.where((c >= r + halo - w // 2) & (c < r + halo - w // 2 + w), 1.0, 0.0).astype(BF16)
        for s in range(tm // sub):
            hs = hbuf[s * sub:s * sub + sub + 2 * halo, gi * gd:(gi + 1) * gd]
            hi = hs.astype(BF16)
            lo = (hs - hi.astype(F32)).astype(BF16)
            total = (_dot(band, jnp.concatenate([hi, fill], axis=0))
                     + _dot(band, jnp.concatenate([lo, fill], axis=0)))
            t = i * tm + s * sub + t_row
            count = jnp.clip(t - w // 2 + w, 0, seq_len) - jnp.clip(t - w // 2, 0, seq_len)
            centre = hbuf[halo + s * sub:halo + (s + 1) * sub, gi * gd:(gi + 1) * gd]
            diff = total / count.astype(F32) - centre
            y = _dot(diff.astype(BF16), pw_ref[gi]) + pb_ref[:, gi * gd:(gi + 1) * gd]
            ybuf[s * sub:(s + 1) * sub, gi * gd:(gi + 1) * gd] = y * ps_ref[:, gi * gd:(gi + 1) * gd]

    xo = x + mod_ref[0, 2:3, :] * _rmsnorm(ybuf[...], g_ref[1:2, :])
    xo_ref[0] = xo
    h2 = _norm_mod(xo, g_ref[2:3, :], mod_ref[0, 3:4, :], mod_ref[0, 4:5, :])
    h2_ref[0] = h2

    logits = _dot(h2.astype(BF16), rw_ref[...]) + rb_ref[...]
    lane = lax.broadcasted_iota(jnp.int32, logits.shape, 1)
    m1 = jnp.max(logits, axis=-1, keepdims=True)
    i1 = jnp.min(jnp.where(logits == m1, lane, LANES), axis=-1, keepdims=True)
    rest = jnp.where(lane == i1, NEG_INF * 2, logits)
    m2 = jnp.max(rest, axis=-1, keepdims=True)
    i2 = jnp.min(jnp.where(rest == m2, lane, LANES), axis=-1, keepdims=True)
    e = jnp.exp(m2 - m1)
    g1 = 1.0 / (1.0 + e)
    g2 = e / (1.0 + e)
    route_ref[0] = jnp.where(lane == 0, i1.astype(F32),
                             jnp.where(lane == 1, i2.astype(F32),
                                       jnp.where(lane == 2, g1, jnp.where(lane == 3, g2, 0.0))))


def _pool_mixer(x, norm_g, mods, mod_row, pool_w, pool_b, pool_scale, router_w, router_b):
    b, l, d = x.shape
    tm = min(512, l)
    hb = tm // POOL_HALO
    n_halo_blocks = l // POOL_HALO
    row = lambda bi, i: (bi, i, 0)
    const2 = lambda bi, i: (0, 0)
    return pl.pallas_call(
        functools.partial(_pool_kernel, tm=tm, seq_len=l),
        grid=(b, l // tm),
        in_specs=[
            pl.BlockSpec((1, tm, d), row),
            pl.BlockSpec((1, POOL_HALO, d), lambda bi, i: (bi, jnp.maximum(i * hb - 1, 0), 0)),
            pl.BlockSpec((1, POOL_HALO, d), lambda bi, i: (bi, jnp.minimum((i + 1) * hb, n_halo_blocks - 1), 0)),
            pl.BlockSpec(norm_g.shape, const2),
            pl.BlockSpec((1, 6, d), lambda bi, i: (mod_row(bi), 0, 0)),
            pl.BlockSpec(pool_w.shape, lambda bi, i: (0, 0, 0)),
            pl.BlockSpec((1, d), const2),
            pl.BlockSpec((1, d), const2),
            pl.BlockSpec(router_w.shape, const2),
            pl.BlockSpec(router_b.shape, const2),
        ],
        out_specs=[pl.BlockSpec((1, tm, d), row), pl.BlockSpec((1, tm, d), row),
                   pl.BlockSpec((1, tm, LANES), row)],
        out_shape=[jax.ShapeDtypeStruct((b, l, d), F32), jax.ShapeDtypeStruct((b, l, d), F32),
                   jax.ShapeDtypeStruct((b, l, LANES), F32)],
        scratch_shapes=[pltpu.VMEM((tm + 2 * POOL_HALO, d), F32), pltpu.VMEM((tm, d), F32)],
        compiler_params=_cparams(2),
        name="pool_mixer",
    )(x, x, x, norm_g, mods, pool_w, pool_b, pool_scale, router_w, router_b)


def _route_plan(route, tm):
    m = route.shape[0]
    n_rows = TOP_K * m + N_EXPERTS * tm
    n_tiles = n_rows // tm
    expert = route[:, :TOP_K].astype(jnp.int32).reshape(-1)
    onehot = (expert[:, None] == jnp.arange(N_EXPERTS, dtype=jnp.int32)[None, :]).astype(jnp.int32)
    csum = jnp.cumsum(onehot, axis=0)
    counts = csum[-1]
    rank = jnp.sum(csum * onehot, axis=1) - 1
    padded = ((counts + tm - 1) // tm) * tm
    ends = jnp.cumsum(padded)
    starts = ends - padded
    pos = jnp.sum(starts[None, :] * onehot, axis=1) + rank
    tok_of_row = jnp.zeros((n_rows,), jnp.int32).at[pos].set(jnp.arange(TOP_K * m, dtype=jnp.int32) // TOP_K)
    tile_start = jnp.arange(n_tiles, dtype=jnp.int32) * tm
    tile_expert = jnp.minimum(jnp.sum((tile_start[:, None] >= ends[None, :]).astype(jnp.int32), axis=1),
                              N_EXPERTS - 1)
    n_used = (ends[-1] // tm).reshape(1).astype(jnp.int32)
    return pos.astype(jnp.int32), tok_of_row, tile_expert.astype(jnp.int32), n_used


def _row_gather(idx_ref, base, stride, n, src_hbm, dst_ref, sem):
    def issue(r, carry):
        t = idx_ref[base + stride * r]
        pltpu.make_async_copy(src_hbm.at[pl.ds(t, 1)], dst_ref.at[pl.ds(r, 1)], sem).start()
        return carry
    lax.fori_loop(0, n, issue, 0, unroll=8)


def _row_gather_wait(n, src_hbm, dst_ref, sem):
    def wait(r, carry):
        pltpu.make_async_copy(src_hbm.at[pl.ds(0, 1)], dst_ref.at[pl.ds(r, 1)], sem).wait()
        return carry
    lax.fori_loop(0, n, wait, 0, unroll=8)


def _dispatch_kernel(tok_ref, nused_ref, h_hbm, xs_ref, sem, *, tm):
    i = pl.program_id(0)

    @pl.when(i < nused_ref[0])
    def _():
        _row_gather(tok_ref, i * tm, 1, tm, h_hbm, xs_ref, sem)
        _row_gather_wait(tm, h_hbm, xs_ref, sem)

    @pl.when(i >= nused_ref[0])
    def _():
        xs_ref[...] = jnp.zeros_like(xs_ref)


def _dispatch(h2, tok_of_row, n_used, tm):
    n_rows = tok_of_row.shape[0]
    d = h2.shape[1]
    return pl.pallas_call(
        functools.partial(_dispatch_kernel, tm=tm),
        grid_spec=pltpu.PrefetchScalarGridSpec(
            num_scalar_prefetch=2,
            grid=(n_rows // tm,),
            in_specs=[pl.BlockSpec(memory_space=pl.ANY)],
            out_specs=pl.BlockSpec((tm, d), lambda i, tok, nu: (i, 0)),
            scratch_shapes=[pltpu.SemaphoreType.DMA],
        ),
        out_shape=jax.ShapeDtypeStruct((n_rows, d), h2.dtype),
        compiler_params=_cparams(1),
        name="moe_dispatch",
    )(tok_of_row, n_used, h2)


def _experts_kernel(te_ref, nused_ref, xs_ref, wg_ref, wu_ref, wd_ref, ys_ref, xb_ref):
    i, j = pl.program_id(0), pl.program_id(1)
    used = i < nused_ref[0]

    @pl.when(j == 0)
    def _():
        ys_ref[...] = jnp.zeros_like(ys_ref)

    @pl.when(used & (j == 0))
    def _():
        xb_ref[...] = xs_ref[...].astype(BF16)

    @pl.when(used)
    def _():
        xb = xb_ref[...]
        act = (_silu(_dot(xb, wg_ref[0])) * _dot(xb, wu_ref[0])).astype(BF16)
        ys_ref[...] += _dot(act, wd_ref[0])


def _experts(xs, tile_expert, n_used, w_gu, w_down, tm):
    n_rows, d = xs.shape
    d_ff = w_down.shape[1]
    tf = 512
    n_chunks = d_ff // tf

    def chunk(i, j, nu):
        return jnp.where(i < nu[0], j, n_chunks - 1)

    return pl.pallas_call(
        _experts_kernel,
        grid_spec=pltpu.PrefetchScalarGridSpec(
            num_scalar_prefetch=2,
            grid=(n_rows // tm, n_chunks),
            in_specs=[
                pl.BlockSpec((tm, d), lambda i, j, te, nu: (i, 0)),
                pl.BlockSpec((1, d, tf), lambda i, j, te, nu: (te[i], 0, chunk(i, j, nu))),
                pl.BlockSpec((1, d, tf), lambda i, j, te, nu: (te[i], 0, n_chunks + chunk(i, j, nu))),
                pl.BlockSpec((1, tf, d), lambda i, j, te, nu: (te[i], chunk(i, j, nu), 0)),
            ],
            out_specs=pl.BlockSpec((tm, d), lambda i, j, te, nu: (i, 0)),
            scratch_shapes=[pltpu.VMEM((tm, d), BF16)],
        ),
        out_shape=jax.ShapeDtypeStruct((n_rows, d), F32),
        compiler_params=_cparams(2),
        name="moe_experts",
    )(tile_expert, n_used, xs, w_gu, w_gu, w_down)


def _combine_kernel(pos_ref, ys_hbm, route_ref, x_ref, g_ref, mod_ref, o_ref, a_buf, b_buf, sem, *, tc):
    base = pl.program_id(0) * tc * TOP_K
    _row_gather(pos_ref, base, TOP_K, tc, ys_hbm, a_buf, sem.at[0])
    _row_gather(pos_ref, base + 1, TOP_K, tc, ys_hbm, b_buf, sem.at[1])
    _row_gather_wait(tc, ys_hbm, a_buf, sem.at[0])
    _row_gather_wait(tc, ys_hbm, b_buf, sem.at[1])
    route = route_ref[...]
    f = route[:, 2:3] * a_buf[...] + route[:, 3:4] * b_buf[...]
    o_ref[...] = x_ref[...] + mod_ref[0, 5:6, :] * _rmsnorm(f, g_ref[3:4, :])


def _combine(ys, pos, route, x, norm_g, mods, mod_row_of_block, tc):
    m, d = x.shape
    return pl.pallas_call(
        functools.partial(_combine_kernel, tc=tc),
        grid_spec=pltpu.PrefetchScalarGridSpec(
            num_scalar_prefetch=1,
            grid=(m // tc,),
            in_specs=[
                pl.BlockSpec(memory_space=pl.ANY),
                pl.BlockSpec((tc, LANES), lambda i, pos: (i, 0)),
                pl.BlockSpec((tc, d), lambda i, pos: (i, 0)),
                pl.BlockSpec(norm_g.shape, lambda i, pos: (0, 0)),
                pl.BlockSpec((1, 6, d), lambda i, pos: (mod_row_of_block(i), 0, 0)),
            ],
            out_specs=pl.BlockSpec((tc, d), lambda i, pos: (i, 0)),
            scratch_shapes=[pltpu.VMEM((tc, d), F32), pltpu.VMEM((tc, d), F32),
                            pltpu.SemaphoreType.DMA((2,))],
        ),
        out_shape=jax.ShapeDtypeStruct((m, d), F32),
        compiler_params=_cparams(1),
        name="moe_combine",
    )(pos, ys, route, x, norm_g, mods)


def _moe(h2, route, x, norm_g, mods, mod_row, w_gu, w_down):
    b, l, d = x.shape
    m = b * l
    tm = min(1024, l)
    tc = min(512, l)
    route2 = route.reshape(m, LANES)
    pos, tok_of_row, tile_expert, n_used = _route_plan(route2, tm)
    xs = _dispatch(h2.reshape(m, d), tok_of_row, n_used, tm)
    ys = _experts(xs, tile_expert, n_used, w_gu, w_down, tm)
    blocks_per_seq = l // tc
    out = _combine(ys, pos, route2, x.reshape(m, d), norm_g, mods,
                   lambda i: mod_row(i // blocks_per_seq), tc)
    return out.reshape(b, l, d)


def kernel(x, c, ctx, c_ctx, ada_w, ada_b, norm_g, attn_w_qkv, attn_w_o, attn_sink, pool_w, pool_b,
           pool_scale, ffn_w_gu, ffn_w_down, router_w, router_b, moe_w_gu, moe_w_down):
    batch, seq, d = x.shape
    depth = ada_w.shape[0]
    ctx_row = batch
    assert batch < MOD_ROWS and seq % ATTN_BLOCK == 0 and d == N_HEADS * HEAD_DIM

    cond = jnp.zeros((MOD_ROWS, d), F32).at[:batch].set(c).at[ctx_row].set(c_ctx)
    mods = _ada_mods(cond, ada_w, ada_b).reshape(depth, MOD_ROWS, 6, d)
    latent_row = lambda bi: bi
    context_row = lambda bi: ctx_row
    rope = _rope_tables(seq)

    w_qkv = attn_w_qkv.astype(BF16)
    w_o = attn_w_o.astype(BF16)
    w_gu = ffn_w_gu.astype(BF16)
    w_down = ffn_w_down.astype(BF16)
    p_w = pool_w.astype(BF16)
    e_gu = moe_w_gu.astype(BF16)
    e_down = moe_w_down.astype(BF16)
    n_exp = router_w.shape[2]
    r_w = jnp.zeros((router_w.shape[0], d, LANES), BF16).at[:, :, :n_exp].set(router_w.astype(BF16))
    r_b = jnp.full((router_b.shape[0], 1, LANES), NEG_INF, F32).at[:, 0, :n_exp].set(router_b)

    is_attn = [i % 2 == 0 for i in range(depth)]
    for i in range(depth):
        slot = i // 2
        ctx_later = any(is_attn[i + 1:])
        g, m = norm_g[i], mods[i]
        if is_attn[i]:
            q, kv = _qkv_proj(x, g, m, latent_row, w_qkv[slot], rope)
            qc, kvc = _qkv_proj(ctx, g, m, context_row, w_qkv[slot], None)
            attn = _attention(q, kv, kvc, attn_sink[slot], True)
            x = _oproj_ffn(attn, x, g, m, latent_row, w_o[slot], w_gu[slot], w_down[slot])
            if ctx_later:
                attn_c = _attention(qc, None, kvc, attn_sink[slot], False)
                ctx = _oproj_ffn(attn_c, ctx, g, m, context_row, w_o[slot], w_gu[slot], w_down[slot])
        else:
            pb, ps = pool_b[slot].reshape(1, d), pool_scale[slot].reshape(1, d)
            x1, h2, route = _pool_mixer(x, g, m, latent_row, p_w[slot], pb, ps, r_w[slot], r_b[slot])
            x = _moe(h2, route, x1, g, m, latent_row, e_gu[slot], e_down[slot])
            if ctx_later:
                c1, h2c, route_c = _pool_mixer(ctx, g, m, context_row, p_w[slot], pb, ps, r_w[slot], r_b[slot])
                ctx = _moe(h2c, route_c, c1, g, m, context_row, e_gu[slot], e_down[slot])
    return x
```

```python
import functools

import jax
import jax.numpy as jnp
from jax import lax
from jax.experimental import pallas as pl
from jax.experimental.pallas import tpu as pltpu

F32 = jnp.float32
BF16 = jnp.bfloat16

N_HEADS = 16
N_KV_HEADS = 4
HEAD_DIM = 64
GROUP = N_HEADS // N_KV_HEADS
ATTN_BLOCK = 128
GRID_W = 64
ROPE_BASE = 10000.0
ROPE_HALF = HEAD_DIM // 4
POOL_WINDOWS = (2, 4, 8, 16)
POOL_HALO = 8
N_EXPERTS = 8
TOP_K = 2
NORM_EPS = 1e-6
NEG_INF = -1e30
LANES = 128
SUBLANES = 8
MOD_ROWS = 8
VMEM_LIMIT = 56 * 1024 * 1024


def _cparams(n_axes):
    return pltpu.CompilerParams(dimension_semantics=("arbitrary",) * n_axes,
                                vmem_limit_bytes=VMEM_LIMIT)


def _rmsnorm(x, g):
    return x * lax.rsqrt(jnp.mean(x * x, axis=-1, keepdims=True) + NORM_EPS) * g


def _norm_mod(x, g, shift, scale):
    return _rmsnorm(x, g) * (1.0 + scale) + shift


def _silu(x):
    return x / (1.0 + jnp.exp(-x))


def _dot(a, b):
    return jnp.dot(a, b, preferred_element_type=F32)


def _norm_spec(norm_g, layer, n_grid):
    return pl.BlockSpec((1,) + norm_g.shape[1:], lambda *_: (layer, 0, 0))


def _mod_spec(mods, layer, row_of):
    return pl.BlockSpec((1, 1) + mods.shape[2:], lambda *idx: (layer, row_of(*idx), 0, 0))


def _slot_spec(w, slot, **kw):
    zeros = (0,) * (w.ndim - 1)
    return pl.BlockSpec((1,) + w.shape[1:], lambda *_: (slot,) + zeros, **kw)


def _ada_kernel(cond_ref, w_ref, b_ref, o_ref):
    s = _silu(cond_ref[...]).astype(BF16)
    o_ref[0] = _dot(s, w_ref[0].astype(BF16)) + b_ref[0]


def _ada_mods(cond, ada_w, ada_b):
    depth, d, n = ada_w.shape
    tn = 1536
    return pl.pallas_call(
        _ada_kernel,
        grid=(depth, n // tn),
        in_specs=[
            pl.BlockSpec((MOD_ROWS, d), lambda l, j: (0, 0)),
            pl.BlockSpec((1, d, tn), lambda l, j: (l, 0, j)),
            pl.BlockSpec((1, 1, tn), lambda l, j: (l, 0, j)),
        ],
        out_specs=pl.BlockSpec((1, MOD_ROWS, tn), lambda l, j: (l, 0, j)),
        out_shape=jax.ShapeDtypeStruct((depth, MOD_ROWS, n), F32),
        compiler_params=_cparams(2),
        name="ada_mods",
    )(cond, ada_w, ada_b.reshape(depth, 1, n))


def _rope(chunk, cos, sin, take_upper):
    rot = jnp.where(take_upper, pltpu.roll(chunk, LANES - ROPE_HALF, axis=1),
                    pltpu.roll(chunk, ROPE_HALF, axis=1))
    return chunk * cos + rot * sin


def _qkv_kernel(*refs, rope):
    if rope:
        x_ref, g_ref, mod_ref, w_ref, cos_ref, sin_ref, q_ref, kv_ref = refs
    else:
        x_ref, g_ref, mod_ref, w_ref, q_ref, kv_ref = refs
    h = _norm_mod(x_ref[0], g_ref[0, 0:1, :], mod_ref[0, 0, 0:1, :], mod_ref[0, 0, 1:2, :])
    y = _dot(h.astype(BF16), w_ref[0])
    q_dim = N_HEADS * HEAD_DIM
    kv_dim = N_KV_HEADS * HEAD_DIM
    if rope:
        cos, sin = cos_ref[...], sin_ref[...]
        lane = lax.broadcasted_iota(jnp.int32, cos.shape, 1)
        take_upper = (lane % (2 * ROPE_HALF)) < ROPE_HALF
    for c in range(q_dim // LANES):
        chunk = y[:, c * LANES:(c + 1) * LANES]
        if rope:
            chunk = _rope(chunk, cos, sin, take_upper)
        q_ref[0, :, c * LANES:(c + 1) * LANES] = (chunk * HEAD_DIM ** -0.5).astype(BF16)
    for c in range(kv_dim // LANES):
        k = y[:, q_dim + c * LANES:q_dim + (c + 1) * LANES]
        if rope:
            k = _rope(k, cos, sin, take_upper)
        v = y[:, q_dim + kv_dim + c * LANES:q_dim + kv_dim + (c + 1) * LANES]
        kv_ref[0, :, c * LANES:(c + 1) * LANES] = k.astype(BF16)
        kv_ref[0, :, kv_dim + c * LANES:kv_dim + (c + 1) * LANES] = pltpu.roll(k, HEAD_DIM, axis=1).astype(BF16)
        kv_ref[0, :, 2 * kv_dim + c * LANES:2 * kv_dim + (c + 1) * LANES] = v.astype(BF16)
        kv_ref[0, :, 3 * kv_dim + c * LANES:3 * kv_dim + (c + 1) * LANES] = pltpu.roll(v, HEAD_DIM, axis=1).astype(BF16)


def _qkv_proj(x, norm_g, mods, layer, mod_row, w_qkv, slot, rope_tables):
    b, l, d = x.shape
    tm = min(512, l)
    rope = rope_tables is not None
    in_specs = [
        pl.BlockSpec((1, tm, d), lambda bi, i: (bi, i, 0)),
        _norm_spec(norm_g, layer, 2),
        _mod_spec(mods, layer, lambda bi, i: mod_row(bi)),
        _slot_spec(w_qkv, slot),
    ]
    args = [x, norm_g, mods, w_qkv]
    if rope:
        in_specs += [pl.BlockSpec((tm, LANES), lambda bi, i: (i, 0))] * 2
        args += list(rope_tables)
    q_dim = N_HEADS * HEAD_DIM
    return pl.pallas_call(
        functools.partial(_qkv_kernel, rope=rope),
        grid=(b, l // tm),
        in_specs=in_specs,
        out_specs=[pl.BlockSpec((1, tm, q_dim), lambda bi, i: (bi, i, 0)),
                   pl.BlockSpec((1, tm, q_dim), lambda bi, i: (bi, i, 0))],
        out_shape=[jax.ShapeDtypeStruct((b, l, q_dim), BF16),
                   jax.ShapeDtypeStruct((b, l, q_dim), BF16)],
        compiler_params=_cparams(2),
        name="qkv_rope" if rope else "qkv",
    )(*args)


def _rope_tables(n_tokens):
    t = jnp.arange(n_tokens)
    row = (t // GRID_W).astype(F32)
    col = (t % GRID_W).astype(F32)
    inv_freq = ROPE_BASE ** (-jnp.arange(ROPE_HALF, dtype=F32) / ROPE_HALF)
    ang_r = row[:, None] * inv_freq
    ang_c = col[:, None] * inv_freq
    cos_h = jnp.concatenate([jnp.cos(ang_r), jnp.cos(ang_r), jnp.cos(ang_c), jnp.cos(ang_c)], axis=-1)
    sin_h = jnp.concatenate([-jnp.sin(ang_r), jnp.sin(ang_r), -jnp.sin(ang_c), jnp.sin(ang_c)], axis=-1)
    reps = LANES // HEAD_DIM
    return jnp.tile(cos_h, (1, reps)), jnp.tile(sin_h, (1, reps))


def _pad_heads(dst_ref, rows, src):
    kv_dim = N_KV_HEADS * HEAD_DIM
    n = src.shape[0]
    lane = lax.broadcasted_iota(jnp.int32, (n, LANES), 1).astype(F32).astype(BF16)
    low = lane < HEAD_DIM
    zero = jnp.zeros((n, LANES), BF16)
    for t in range(2):
        for h in range(N_KV_HEADS):
            c = h // 2
            plain = src[:, 2 * t * kv_dim + c * LANES:2 * t * kv_dim + (c + 1) * LANES]
            swapped = src[:, (2 * t + 1) * kv_dim + c * LANES:(2 * t + 1) * kv_dim + (c + 1) * LANES]
            in_low, in_high = (plain, swapped) if h % 2 == 0 else (swapped, plain)
            base = t * 2 * N_KV_HEADS * LANES + h * 2 * LANES
            dst_ref[rows, base:base + LANES] = jnp.where(low, in_low, zero)
            dst_ref[rows, base + LANES:base + 2 * LANES] = jnp.where(low, zero, in_high)


def _lane_chunks(a):
    return [a[:, c * LANES:(c + 1) * LANES] for c in range(a.shape[1] // LANES)]


def _attn_kernel(*refs, tq, n_local_blocks, has_local, slot):
    if has_local:
        sink_ref, q_ref, kvo_ref, kvp_ref, kvn_ref, kvc_ref, o_ref, loc_pad, ctx_pad = refs
    else:
        sink_ref, q_ref, kvc_ref, o_ref, ctx_pad = refs
    blk = ATTN_BLOCK
    i = pl.program_id(1)
    n_ctx = kvc_ref.shape[1]
    v_base = 2 * N_KV_HEADS * LANES
    _pad_heads(ctx_pad, slice(0, n_ctx), kvc_ref[0])
    if has_local:
        _pad_heads(loc_pad, slice(0, blk), kvp_ref[0])
        _pad_heads(loc_pad, slice(blk, blk + tq), kvo_ref[0])
        _pad_heads(loc_pad, slice(blk + tq, 2 * blk + tq), kvn_ref[0])

    lane_low = lax.broadcasted_iota(jnp.int32, (2 * blk, LANES), 1) < HEAD_DIM
    row_first = lax.broadcasted_iota(jnp.int32, (2 * blk, 1), 0) < blk

    def q_block(j, carry):
        r0 = pl.multiple_of(j * blk, blk)
        if has_local:
            n = i * (tq // blk) + j
            qi = lax.broadcasted_iota(jnp.int32, (2 * blk, 3 * blk), 0) & (blk - 1)
            ki = lax.broadcasted_iota(jnp.int32, (2 * blk, 3 * blk), 1)
            rel = ki - blk - qi
            k_lo = jnp.where(n == 0, blk, 0)
            k_hi = jnp.where(n == n_local_blocks - 1, 2 * blk, 3 * blk)
            valid = (rel >= -blk) & (rel <= blk) & (ki >= k_lo) & (ki < k_hi)
        for h in range(N_KV_HEADS):
            qp = q_ref[0, pl.ds(r0, blk), h * 2 * LANES:(h + 1) * 2 * LANES]
            lhs = jnp.concatenate([qp[:, :LANES], qp[:, LANES:]], axis=0)
            kc = h * 2 * LANES
            vc = v_base + h * 2 * LANES
            nt = (((1,), (1,)), ((), ()))
            seg_scores = [[], []]
            k_ctx = jnp.concatenate([ctx_pad[:, kc:kc + LANES], ctx_pad[:, kc + LANES:kc + 2 * LANES]], axis=0)
            s_ctx = lax.dot_general(lhs, k_ctx, nt, preferred_element_type=F32)
            seg_scores[0].append(s_ctx[:, :n_ctx])
            seg_scores[1].append(s_ctx[:, n_ctx:])
            if has_local:
                k_loc = jnp.concatenate([loc_pad[pl.ds(r0, 3 * blk), kc:kc + LANES],
                                         loc_pad[pl.ds(r0, 3 * blk), kc + LANES:kc + 2 * LANES]], axis=0)
                s_loc = lax.dot_general(lhs, k_loc, nt, preferred_element_type=F32)
                seg_scores[0].append(jnp.where(valid, s_loc[:, :3 * blk], NEG_INF))
                seg_scores[1].append(jnp.where(valid, s_loc[:, 3 * blk:], NEG_INF))
            probs, inv = [], []
            for seg in range(2):
                sink = jnp.where(row_first, sink_ref[slot, h * GROUP + seg], sink_ref[slot, h * GROUP + 2 + seg])
                col_max = functools.reduce(jnp.maximum, [c for s in seg_scores[seg] for c in _lane_chunks(s)])
                m = jnp.maximum(sink, jnp.max(col_max, axis=-1, keepdims=True))
                ps = [jnp.exp(s - m) for s in seg_scores[seg]]
                col_sum = functools.reduce(jnp.add, [c for p in ps for c in _lane_chunks(p)])
                denom = jnp.exp(sink - m) + jnp.sum(col_sum, axis=-1, keepdims=True)
                probs.append(ps)
                inv.append(1.0 / denom)
            p_ctx = jnp.concatenate([probs[0][0], probs[1][0]], axis=1).astype(BF16)
            v_ctx = jnp.concatenate([ctx_pad[:, vc:vc + LANES], ctx_pad[:, vc + LANES:vc + 2 * LANES]], axis=0)
            o = _dot(p_ctx, v_ctx)
            if has_local:
                p_loc = jnp.concatenate([probs[0][1], probs[1][1]], axis=1).astype(BF16)
                v_loc = jnp.concatenate([loc_pad[pl.ds(r0, 3 * blk), vc:vc + LANES],
                                         loc_pad[pl.ds(r0, 3 * blk), vc + LANES:vc + 2 * LANES]], axis=0)
                o = o + _dot(p_loc, v_loc)
            o = o * jnp.where(lane_low, inv[0], inv[1])
            o_ref[0, pl.ds(r0, blk), h * 2 * LANES:h * 2 * LANES + LANES] = o[:blk].astype(BF16)
            o_ref[0, pl.ds(r0, blk), h * 2 * LANES + LANES:(h + 1) * 2 * LANES] = o[blk:].astype(BF16)
        return carry

    lax.fori_loop(0, tq // blk, q_block, 0)


def _attention(q, kv, kv_ctx, sink, slot, has_local):
    b, l, q_dim = q.shape
    n_ctx = kv_ctx.shape[1]
    blk = ATTN_BLOCK
    tq = min(1024, l)
    bpt = tq // blk
    n_blocks = l // blk
    pad_lanes = 4 * N_KV_HEADS * LANES
    in_specs = [pl.BlockSpec(memory_space=pltpu.SMEM),
                pl.BlockSpec((1, tq, q_dim), lambda bi, i: (bi, i, 0))]
    args = [sink, q]
    scratch = []
    if has_local:
        in_specs += [
            pl.BlockSpec((1, tq, q_dim), lambda bi, i: (bi, i, 0)),
            pl.BlockSpec((1, blk, q_dim), lambda bi, i: (bi, jnp.maximum(i * bpt - 1, 0), 0)),
            pl.BlockSpec((1, blk, q_dim), lambda bi, i: (bi, jnp.minimum((i + 1) * bpt, n_blocks - 1), 0)),
        ]
        args += [kv, kv, kv]
        scratch.append(pltpu.VMEM((tq + 2 * blk, pad_lanes), BF16))
    in_specs.append(pl.BlockSpec((1, n_ctx, q_dim), lambda bi, i: (bi, 0, 0)))
    args.append(kv_ctx)
    scratch.append(pltpu.VMEM((n_ctx, pad_lanes), BF16))
    return pl.pallas_call(
        functools.partial(_attn_kernel, tq=tq, n_local_blocks=n_blocks, has_local=has_local, slot=slot),
        grid=(b, l // tq),
        in_specs=in_specs,
        out_specs=pl.BlockSpec((1, tq, q_dim), lambda bi, i: (bi, i, 0)),
        out_shape=jax.ShapeDtypeStruct((b, l, q_dim), BF16),
        scratch_shapes=scratch,
        compiler_params=_cparams(2),
        name="window_attention" if has_local else "context_attention",
    )(*args)


def _oproj_ffn_kernel(a_ref, x_ref, g_ref, mod_ref, wo_ref, wgu_ref, wd_ref, o_ref):
    d_ff = wd_ref.shape[1]
    mod = lambda r: mod_ref[0, 0, r:r + 1, :]
    y = _dot(a_ref[0], wo_ref[0])
    x = x_ref[0] + mod(2) * _rmsnorm(y, g_ref[0, 1:2, :])
    h = _norm_mod(x, g_ref[0, 2:3, :], mod(3), mod(4)).astype(BF16)
    gate = _dot(h, wgu_ref[0, :, :d_ff])
    up = _dot(h, wgu_ref[0, :, d_ff:])
    f = _dot((_silu(gate) * up).astype(BF16), wd_ref[0])
    o_ref[0] = x + mod(5) * _rmsnorm(f, g_ref[0, 3:4, :])


def _oproj_ffn(attn, x, norm_g, mods, layer, mod_row, w_o, w_gu, w_down, slot):
    b, l, d = x.shape
    tm = min(512, l)
    resident = pl.Buffered(1)
    return pl.pallas_call(
        _oproj_ffn_kernel,
        grid=(b, l // tm),
        in_specs=[
            pl.BlockSpec((1, tm, attn.shape[2]), lambda bi, i: (bi, i, 0)),
            pl.BlockSpec((1, tm, d), lambda bi, i: (bi, i, 0)),
            _norm_spec(norm_g, layer, 2),
            _mod_spec(mods, layer, lambda bi, i: mod_row(bi)),
            _slot_spec(w_o, slot, pipeline_mode=resident),
            _slot_spec(w_gu, slot, pipeline_mode=resident),
            _slot_spec(w_down, slot, pipeline_mode=resident),
        ],
        out_specs=pl.BlockSpec((1, tm, d), lambda bi, i: (bi, i, 0)),
        out_shape=jax.ShapeDtypeStruct((b, l, d), F32),
        compiler_params=_cparams(2),
        name="oproj_ffn",
    )(attn, x, norm_g, mods, w_o, w_gu, w_down)


def _pool_kernel(x_ref, xp_ref, xn_ref, g_ref, mod_ref, pw_ref, pb_ref, ps_ref, rw_ref, rb_ref,
                 xo_ref, h2_ref, route_ref, hbuf, ybuf, *, tm, seq_len):
    i = pl.program_id(1)
    n_blk = pl.num_programs(1)
    sub = 128
    halo = POOL_HALO
    gd = pw_ref.shape[2]
    mod = lambda r: mod_ref[0, 0, r:r + 1, :]
    g_pre, shift, scale = g_ref[0, 0:1, :], mod(0), mod(1)
    x = x_ref[0]
    hbuf[0:halo, :] = _norm_mod(xp_ref[0], g_pre, shift, scale) * jnp.where(i > 0, 1.0, 0.0)
    hbuf[halo:halo + tm, :] = _norm_mod(x, g_pre, shift, scale)
    hbuf[halo + tm:, :] = _norm_mod(xn_ref[0], g_pre, shift, scale) * jnp.where(i < n_blk - 1, 1.0, 0.0)

    r = lax.broadcasted_iota(jnp.int32, (sub, 2 * sub), 0)
    c = lax.broadcasted_iota(jnp.int32, (sub, 2 * sub), 1)
    t_row = lax.broadcasted_iota(jnp.int32, (sub, 1), 0)
    fill = jnp.zeros((sub - 2 * halo, gd), BF16)
    for gi, w in enumerate(POOL_WINDOWS):
        band = jnp.where((c >= r + halo - w // 2) & (c < r + halo - w // 2 + w), 1.0, 0.0).astype(BF16)
        for s in range(tm // sub):
            hs = hbuf[s * sub:s * sub + sub + 2 * halo, gi * gd:(gi + 1) * gd]
            hi = hs.astype(BF16)
            lo = (hs - hi.astype(F32)).astype(BF16)
            total = (_dot(band, jnp.concatenate([hi, fill], axis=0))
                     + _dot(band, jnp.concatenate([lo, fill], axis=0)))
            t = i * tm + s * sub + t_row
            count = jnp.clip(t - w // 2 + w, 0, seq_len) - jnp.clip(t - w // 2, 0, seq_len)
            centre = hbuf[halo + s * sub:halo + (s + 1) * sub, gi * gd:(gi + 1) * gd]
            diff = total / count.astype(F32) - centre
            y = _dot(diff.astype(BF16), pw_ref[0, gi]) + pb_ref[0, :, gi * gd:(gi + 1) * gd]
            ybuf[s * sub:(s + 1) * sub, gi * gd:(gi + 1) * gd] = y * ps_ref[0, :, gi * gd:(gi + 1) * gd]

    xo = x + mod(2) * _rmsnorm(ybuf[...], g_ref[0, 1:2, :])
    xo_ref[0] = xo
    h2 = _norm_mod(xo, g_ref[0, 2:3, :], mod(3), mod(4))
    for s in range(SUBLANES):
        h2_ref[pl.ds(s, tm, stride=SUBLANES), :] = h2[:, s * LANES:(s + 1) * LANES]

    logits = _dot(h2.astype(BF16), rw_ref[0]) + rb_ref[0]
    lane = lax.broadcasted_iota(jnp.int32, logits.shape, 1)
    m1 = jnp.max(logits, axis=-1, keepdims=True)
    i1 = jnp.min(jnp.where(logits == m1, lane, LANES), axis=-1, keepdims=True)
    rest = jnp.where(lane == i1, NEG_INF * 2, logits)
    m2 = jnp.max(rest, axis=-1, keepdims=True)
    i2 = jnp.min(jnp.where(rest == m2, lane, LANES), axis=-1, keepdims=True)
    e = jnp.exp(m2 - m1)
    g1 = 1.0 / (1.0 + e)
    g2 = e / (1.0 + e)
    route_ref[0] = jnp.where(lane == 0, i1.astype(F32),
                             jnp.where(lane == 1, i2.astype(F32),
                                       jnp.where(lane == 2, g1, jnp.where(lane == 3, g2, 0.0))))


def _pool_mixer(x, norm_g, mods, layer, mod_row, pool_w, pool_b, pool_scale, router_w, router_b, slot):
    b, l, d = x.shape
    assert d == SUBLANES * LANES
    tm = min(512, l)
    hb = tm // POOL_HALO
    n_halo_blocks = l // POOL_HALO
    n_blk = l // tm
    row = lambda bi, i: (bi, i, 0)
    return pl.pallas_call(
        functools.partial(_pool_kernel, tm=tm, seq_len=l),
        grid=(b, n_blk),
        in_specs=[
            pl.BlockSpec((1, tm, d), row),
            pl.BlockSpec((1, POOL_HALO, d), lambda bi, i: (bi, jnp.maximum(i * hb - 1, 0), 0)),
            pl.BlockSpec((1, POOL_HALO, d), lambda bi, i: (bi, jnp.minimum((i + 1) * hb, n_halo_blocks - 1), 0)),
            _norm_spec(norm_g, layer, 2),
            _mod_spec(mods, layer, lambda bi, i: mod_row(bi)),
            _slot_spec(pool_w, slot),
            _slot_spec(pool_b, slot),
            _slot_spec(pool_scale, slot),
            _slot_spec(router_w, slot),
            _slot_spec(router_b, slot),
        ],
        out_specs=[pl.BlockSpec((1, tm, d), row),
                   pl.BlockSpec((tm * SUBLANES, LANES), lambda bi, i: (bi * n_blk + i, 0)),
                   pl.BlockSpec((1, tm, LANES), row)],
        out_shape=[jax.ShapeDtypeStruct((b, l, d), F32),
                   jax.ShapeDtypeStruct((b * l * SUBLANES, LANES), F32),
                   jax.ShapeDtypeStruct((b, l, LANES), F32)],
        scratch_shapes=[pltpu.VMEM((tm + 2 * POOL_HALO, d), F32), pltpu.VMEM((tm, d), F32)],
        compiler_params=_cparams(2),
        name="pool_mixer",
    )(x, x, x, norm_g, mods, pool_w, pool_b, pool_scale, router_w, router_b)


def _route_plan(route, tm):
    m = route.shape[0]
    n_tiles = (TOP_K * m) // tm + N_EXPERTS
    expert = route[:, :TOP_K].astype(jnp.int32).reshape(-1)
    onehot = (expert[:, None] == jnp.arange(N_EXPERTS, dtype=jnp.int32)[None, :]).astype(jnp.int32)
    csum = jnp.cumsum(onehot, axis=0)
    counts = csum[-1]
    rank = jnp.sum(csum * onehot, axis=1) - 1
    padded = ((counts + tm - 1) // tm) * tm
    ends = jnp.cumsum(padded)
    starts = ends - padded
    pos = jnp.sum(starts[None, :] * onehot, axis=1) + rank
    tile_start = jnp.arange(n_tiles, dtype=jnp.int32) * tm
    tile_expert = jnp.minimum(jnp.sum((tile_start[:, None] >= ends[None, :]).astype(jnp.int32), axis=1),
                              N_EXPERTS - 1)
    n_used = (ends[-1] // tm).reshape(1)
    i32 = lambda a: a.astype(jnp.int32)
    return i32(pos), i32(tile_expert), i32(n_used), i32(jnp.concatenate([starts, ends, n_used]))


def _tile(ref, row):
    return ref.at[pl.ds(pl.multiple_of(row * SUBLANES, SUBLANES), SUBLANES)]


def _wait_tiles(n, src_ref, dst_ref, sem):
    def wait(r, carry):
        pltpu.make_async_copy(_tile(src_ref, 0), _tile(dst_ref, 0), sem).wait()
        return carry
    lax.fori_loop(0, n, wait, 0, unroll=8)


def _dispatch_kernel(pos_ref, seg_ref, h_ref, xs_hbm, zeros_ref, sem, *, tc, tm):
    i = pl.program_id(0)
    zero_rows = zeros_ref.shape[0] // SUBLANES
    n_tiles = xs_hbm.shape[0] // (tm * SUBLANES)

    def clear_tile(first_row):
        for c in range(tm // zero_rows):
            first = pl.multiple_of((first_row + c * zero_rows) * SUBLANES, SUBLANES)
            cp = pltpu.make_async_copy(zeros_ref, xs_hbm.at[pl.ds(first, zero_rows * SUBLANES)], sem)
            cp.start()
            cp.wait()

    @pl.when(i == 0)
    def _():
        zeros_ref[...] = jnp.zeros_like(zeros_ref)
        for e in range(N_EXPERTS):
            start, end = seg_ref[e], seg_ref[N_EXPERTS + e]
            pl.when(end > start)(functools.partial(clear_tile, end - tm))
            spare = seg_ref[2 * N_EXPERTS] + e
            pl.when(spare < n_tiles)(functools.partial(clear_tile, spare * tm))

    base = i * tc * TOP_K

    def issue(r, carry):
        for k in range(TOP_K):
            p = pos_ref[base + TOP_K * r + k]
            pltpu.make_async_copy(_tile(h_ref, r), _tile(xs_hbm, p), sem).start(priority=k)
        return carry
    lax.fori_loop(0, tc, issue, 0, unroll=8)
    _wait_tiles(TOP_K * tc, h_ref, xs_hbm, sem)


def _dispatch(h2t, pos, seg, n_rows, tm):
    m = h2t.shape[0] // SUBLANES
    tc = min(1024, m)
    return pl.pallas_call(
        functools.partial(_dispatch_kernel, tc=tc, tm=tm),
        grid_spec=pltpu.PrefetchScalarGridSpec(
            num_scalar_prefetch=2,
            grid=(m // tc,),
            in_specs=[pl.BlockSpec((tc * SUBLANES, LANES), lambda i, pos, seg: (i, 0))],
            out_specs=pl.BlockSpec(memory_space=pl.ANY),
            scratch_shapes=[pltpu.VMEM((min(tm, 256) * SUBLANES, LANES), F32), pltpu.SemaphoreType.DMA],
        ),
        out_shape=jax.ShapeDtypeStruct((n_rows * SUBLANES, LANES), F32),
        compiler_params=_cparams(1),
        name="moe_dispatch",
    )(pos, seg, h2t)


def _experts_kernel(te_ref, nused_ref, xs_ref, wg_ref, wu_ref, wd_ref, ys_ref, xb_ref, acc_ref, *, tm):
    i, j = pl.program_id(0), pl.program_id(1)
    used = i < nused_ref[0]
    last = pl.num_programs(1) - 1

    @pl.when(used & (j == 0))
    def _():
        for s in range(SUBLANES):
            xb_ref[:, s * LANES:(s + 1) * LANES] = xs_ref[pl.ds(s, tm, stride=SUBLANES), :].astype(BF16)
        acc_ref[...] = jnp.zeros_like(acc_ref)

    @pl.when(used)
    def _():
        xb = xb_ref[...]
        act = (_silu(_dot(xb, wg_ref[0, 0])) * _dot(xb, wu_ref[0, 0])).astype(BF16)
        acc_ref[...] += _dot(act, wd_ref[0, 0])

    @pl.when(used & (j == last))
    def _():
        for s in range(SUBLANES):
            ys_ref[pl.ds(s, tm, stride=SUBLANES), :] = acc_ref[:, s * LANES:(s + 1) * LANES]

    @pl.when(jnp.logical_not(used) & (j == last))
    def _():
        ys_ref[...] = jnp.zeros_like(ys_ref)


def _experts(xs, tile_expert, n_used, w_gu, w_down, slot, tm):
    n_rows = xs.shape[0] // SUBLANES
    d = SUBLANES * LANES
    d_ff = w_down.shape[2]
    tf = 512
    n_chunks = d_ff // tf

    def chunk(i, j, nu):
        return jnp.where(i < nu[0], j, n_chunks - 1)

    return pl.pallas_call(
        functools.partial(_experts_kernel, tm=tm),
        grid_spec=pltpu.PrefetchScalarGridSpec(
            num_scalar_prefetch=2,
            grid=(n_rows // tm, n_chunks),
            in_specs=[
                pl.BlockSpec((tm * SUBLANES, LANES), lambda i, j, te, nu: (i, 0)),
                pl.BlockSpec((1, 1, d, tf), lambda i, j, te, nu: (slot, te[i], 0, chunk(i, j, nu))),
                pl.BlockSpec((1, 1, d, tf), lambda i, j, te, nu: (slot, te[i], 0, n_chunks + chunk(i, j, nu))),
                pl.BlockSpec((1, 1, tf, d), lambda i, j, te, nu: (slot, te[i], chunk(i, j, nu), 0)),
            ],
            out_specs=pl.BlockSpec((tm * SUBLANES, LANES), lambda i, j, te, nu: (i, 0)),
            scratch_shapes=[pltpu.VMEM((tm, d), BF16), pltpu.VMEM((tm, d), F32)],
        ),
        out_shape=jax.ShapeDtypeStruct((n_rows * SUBLANES, LANES), F32),
        compiler_params=_cparams(2),
        name="moe_experts",
    )(tile_expert, n_used, xs, w_gu, w_gu, w_down)


def _combine_kernel(pos_ref, ys_hbm, route_ref, x_ref, g_ref, mod_ref, o_ref, buf, sem, *, tc):
    i = pl.program_id(0)
    n = pl.num_programs(0)

    def gather(block, slot):
        base = block * tc * TOP_K

        def issue(r, carry):
            for k in range(TOP_K):
                p = pos_ref[base + TOP_K * r + k]
                pltpu.make_async_copy(_tile(ys_hbm, p), _tile(buf.at[slot, k], r), sem.at[slot]).start(priority=k)
            return carry
        lax.fori_loop(0, tc, issue, 0, unroll=8)

    @pl.when(i == 0)
    def _():
        gather(0, 0)

    @pl.when(i + 1 < n)
    def _():
        gather(i + 1, (i + 1) % 2)

    slot = i % 2
    _wait_tiles(TOP_K * tc, ys_hbm, buf.at[slot, 0], sem.at[slot])
    rows = lambda k: jnp.concatenate([buf[slot, k, pl.ds(s, tc, stride=SUBLANES), :] for s in range(SUBLANES)], axis=1)
    route = route_ref[...]
    f = route[:, 2:3] * rows(0) + route[:, 3:4] * rows(1)
    o_ref[...] = x_ref[...] + mod_ref[0, 0, 5:6, :] * _rmsnorm(f, g_ref[0, 3:4, :])


def _combine(ys, pos, route, x, norm_g, mods, layer, mod_row_of_block, tc):
    m, d = x.shape
    return pl.pallas_call(
        functools.partial(_combine_kernel, tc=tc),
        grid_spec=pltpu.PrefetchScalarGridSpec(
            num_scalar_prefetch=1,
            grid=(m // tc,),
            in_specs=[
                pl.BlockSpec(memory_space=pl.ANY),
                pl.BlockSpec((tc, LANES), lambda i, pos: (i, 0)),
                pl.BlockSpec((tc, d), lambda i, pos: (i, 0)),
                _norm_spec(norm_g, layer, 1),
                _mod_spec(mods, layer, lambda i, pos: mod_row_of_block(i)),
            ],
            out_specs=pl.BlockSpec((tc, d), lambda i, pos: (i, 0)),
            scratch_shapes=[pltpu.VMEM((2, TOP_K, tc * SUBLANES, LANES), F32),
                            pltpu.SemaphoreType.DMA((2,))],
        ),
        out_shape=jax.ShapeDtypeStruct((m, d), F32),
        compiler_params=_cparams(1),
        name="moe_combine",
    )(pos, ys, route, x, norm_g, mods)


def _moe(h2t, route, x, norm_g, mods, layer, mod_row, w_gu, w_down, slot):
    b, l, d = x.shape
    m = b * l
    tm = min(1024, l)
    tc = min(512, l)
    route2 = route.reshape(m, LANES)
    pos, tile_expert, n_used, seg = _route_plan(route2, tm)
    n_rows = TOP_K * m + N_EXPERTS * tm
    xs = _dispatch(h2t, pos, seg, n_rows, tm)
    ys = _experts(xs, tile_expert, n_used, w_gu, w_down, slot, tm)
    blocks_per_seq = l // tc
    out = _combine(ys, pos, route2, x.reshape(m, d), norm_g, mods, layer,
                   lambda i: mod_row(i // blocks_per_seq), tc)
    return out.reshape(b, l, d)


def kernel(x, c, ctx, c_ctx, ada_w, ada_b, norm_g, attn_w_qkv, attn_w_o, attn_sink, pool_w, pool_b,
           pool_scale, ffn_w_gu, ffn_w_down, router_w, router_b, moe_w_gu, moe_w_down):
    batch, seq, d = x.shape
    depth = ada_w.shape[0]
    ctx_row = batch
    assert batch < MOD_ROWS and seq % ATTN_BLOCK == 0 and d == N_HEADS * HEAD_DIM

    cond = jnp.zeros((MOD_ROWS, d), F32).at[:batch].set(c).at[ctx_row].set(c_ctx)
    mods = _ada_mods(cond, ada_w, ada_b).reshape(depth, MOD_ROWS, 6, d)
    latent_row = lambda bi: bi
    context_row = lambda bi: ctx_row
    rope = _rope_tables(seq)

    w_qkv = attn_w_qkv.astype(BF16)
    w_o = attn_w_o.astype(BF16)
    w_gu = ffn_w_gu.astype(BF16)
    w_down = ffn_w_down.astype(BF16)
    p_w = pool_w.astype(BF16)
    p_b = pool_b.reshape(-1, 1, d)
    p_s = pool_scale.reshape(-1, 1, d)
    e_gu = moe_w_gu.astype(BF16)
    e_down = moe_w_down.astype(BF16)
    n_exp = router_w.shape[2]
    r_w = jnp.zeros((router_w.shape[0], d, LANES), BF16).at[:, :, :n_exp].set(router_w.astype(BF16))
    r_b = jnp.full((router_b.shape[0], 1, LANES), NEG_INF, F32).at[:, 0, :n_exp].set(router_b)

    is_attn = [i % 2 == 0 for i in range(depth)]
    for i in range(depth):
        slot = i // 2
        ctx_later = any(is_attn[i + 1:])
        if is_attn[i]:
            q, kv = _qkv_proj(x, norm_g, mods, i, latent_row, w_qkv, slot, rope)
            qc, kvc = _qkv_proj(ctx, norm_g, mods, i, context_row, w_qkv, slot, None)
            attn = _attention(q, kv, kvc, attn_sink, slot, True)
            x = _oproj_ffn(attn, x, norm_g, mods, i, latent_row, w_o, w_gu, w_down, slot)
            if ctx_later:
                attn_c = _attention(qc, None, kvc, attn_sink, slot, False)
                ctx = _oproj_ffn(attn_c, ctx, norm_g, mods, i, context_row, w_o, w_gu, w_down, slot)
        else:
            x1, h2t, route = _pool_mixer(x, norm_g, mods, i, latent_row, p_w, p_b, p_s, r_w, r_b, slot)
            x = _moe(h2t, route, x1, norm_g, mods, i, latent_row, e_gu, e_down, slot)
            if ctx_later:
                c1, h2c, route_c = _pool_mixer(ctx, norm_g, mods, i, context_row, p_w, p_b, p_s, r_w, r_b, slot)
                ctx = _moe(h2c, route_c, c1, norm_g, mods, i, context_row, e_gu, e_down, slot)
    return x
```

```python
import functools

import jax
import jax.numpy as jnp
from jax import lax
from jax.experimental import pallas as pl
from jax.experimental.pallas import tpu as pltpu

F32 = jnp.float32
BF16 = jnp.bfloat16

N_HEADS = 16
N_KV_HEADS = 4
HEAD_DIM = 64
GROUP = N_HEADS // N_KV_HEADS
ATTN_BLOCK = 128
GRID_W = 64
ROPE_BASE = 10000.0
ROPE_HALF = HEAD_DIM // 4
POOL_WINDOWS = (2, 4, 8, 16)
POOL_HALO = 8
N_EXPERTS = 8
TOP_K = 2
NORM_EPS = 1e-6
NEG_INF = -1e30
LOG2_E = 1.4426950408889634
Q_SCALE = HEAD_DIM ** -0.5 * LOG2_E
LANES = 128
SUBLANES = 8
MOD_ROWS = 8
VMEM_LIMIT = 56 * 1024 * 1024


def _cparams(n_axes):
    return pltpu.CompilerParams(dimension_semantics=("arbitrary",) * n_axes,
                                vmem_limit_bytes=VMEM_LIMIT)


def _rmsnorm(x, g):
    return x * lax.rsqrt(jnp.mean(x * x, axis=-1, keepdims=True) + NORM_EPS) * g


def _norm_mod(x, g, shift, scale):
    return _rmsnorm(x, g) * (1.0 + scale) + shift


def _silu(x):
    return x / (1.0 + jnp.exp(-x))


def _dot(a, b):
    return jnp.dot(a, b, preferred_element_type=F32)


def _norm_spec(norm_g, layer, n_grid):
    return pl.BlockSpec((1,) + norm_g.shape[1:], lambda *_: (layer, 0, 0))


def _mod_spec(mods, layer, row_of):
    return pl.BlockSpec((1, 1) + mods.shape[2:], lambda *idx: (layer, row_of(*idx), 0, 0))


def _slot_spec(w, slot, **kw):
    zeros = (0,) * (w.ndim - 1)
    return pl.BlockSpec((1,) + w.shape[1:], lambda *_: (slot,) + zeros, **kw)


def _ada_kernel(cond_ref, w_ref, b_ref, o_ref):
    s = _silu(cond_ref[...]).astype(BF16)
    o_ref[0] = _dot(s, w_ref[0].astype(BF16)) + b_ref[0]


def _ada_mods(cond, ada_w, ada_b):
    depth, d, n = ada_w.shape
    tn = 1536
    return pl.pallas_call(
        _ada_kernel,
        grid=(depth, n // tn),
        in_specs=[
            pl.BlockSpec((MOD_ROWS, d), lambda l, j: (0, 0)),
            pl.BlockSpec((1, d, tn), lambda l, j: (l, 0, j)),
            pl.BlockSpec((1, 1, tn), lambda l, j: (l, 0, j)),
        ],
        out_specs=pl.BlockSpec((1, MOD_ROWS, tn), lambda l, j: (l, 0, j)),
        out_shape=jax.ShapeDtypeStruct((depth, MOD_ROWS, n), F32),
        compiler_params=_cparams(2),
        name="ada_mods",
    )(cond, ada_w, ada_b.reshape(depth, 1, n))


def _rope(chunk, cos, sin, take_upper):
    rot = jnp.where(take_upper, pltpu.roll(chunk, LANES - ROPE_HALF, axis=1),
                    pltpu.roll(chunk, ROPE_HALF, axis=1))
    return chunk * cos + rot * sin


def _qkv_kernel(*refs, rope):
    if rope:
        x_ref, g_ref, mod_ref, w_ref, cos_ref, sin_ref, q_ref, kv_ref = refs
    else:
        x_ref, g_ref, mod_ref, w_ref, q_ref, kv_ref = refs
    h = _norm_mod(x_ref[0], g_ref[0, 0:1, :], mod_ref[0, 0, 0:1, :], mod_ref[0, 0, 1:2, :])
    y = _dot(h.astype(BF16), w_ref[0])
    q_dim = N_HEADS * HEAD_DIM
    kv_dim = N_KV_HEADS * HEAD_DIM
    if rope:
        cos, sin = cos_ref[...], sin_ref[...]
        lane = lax.broadcasted_iota(jnp.int32, cos.shape, 1)
        take_upper = (lane % (2 * ROPE_HALF)) < ROPE_HALF
    for c in range(q_dim // LANES):
        chunk = y[:, c * LANES:(c + 1) * LANES]
        if rope:
            chunk = _rope(chunk, cos, sin, take_upper)
        q_ref[0, :, c * LANES:(c + 1) * LANES] = (chunk * Q_SCALE).astype(BF16)
    for c in range(kv_dim // LANES):
        k = y[:, q_dim + c * LANES:q_dim + (c + 1) * LANES]
        if rope:
            k = _rope(k, cos, sin, take_upper)
        v = y[:, q_dim + kv_dim + c * LANES:q_dim + kv_dim + (c + 1) * LANES]
        kv_ref[0, :, c * LANES:(c + 1) * LANES] = k.astype(BF16)
        kv_ref[0, :, kv_dim + c * LANES:kv_dim + (c + 1) * LANES] = pltpu.roll(k, HEAD_DIM, axis=1).astype(BF16)
        kv_ref[0, :, 2 * kv_dim + c * LANES:2 * kv_dim + (c + 1) * LANES] = v.astype(BF16)
        kv_ref[0, :, 3 * kv_dim + c * LANES:3 * kv_dim + (c + 1) * LANES] = pltpu.roll(v, HEAD_DIM, axis=1).astype(BF16)


def _qkv_proj(x, norm_g, mods, layer, mod_row, w_qkv, slot, rope_tables):
    b, l, d = x.shape
    tm = min(512, l)
    rope = rope_tables is not None
    in_specs = [
        pl.BlockSpec((1, tm, d), lambda bi, i: (bi, i, 0)),
        _norm_spec(norm_g, layer, 2),
        _mod_spec(mods, layer, lambda bi, i: mod_row(bi)),
        _slot_spec(w_qkv, slot),
    ]
    args = [x, norm_g, mods, w_qkv]
    if rope:
        in_specs += [pl.BlockSpec((tm, LANES), lambda bi, i: (i, 0))] * 2
        args += list(rope_tables)
    q_dim = N_HEADS * HEAD_DIM
    return pl.pallas_call(
        functools.partial(_qkv_kernel, rope=rope),
        grid=(b, l // tm),
        in_specs=in_specs,
        out_specs=[pl.BlockSpec((1, tm, q_dim), lambda bi, i: (bi, i, 0)),
                   pl.BlockSpec((1, tm, q_dim), lambda bi, i: (bi, i, 0))],
        out_shape=[jax.ShapeDtypeStruct((b, l, q_dim), BF16),
                   jax.ShapeDtypeStruct((b, l, q_dim), BF16)],
        compiler_params=_cparams(2),
        name="qkv_rope" if rope else "qkv",
    )(*args)


def _rope_tables(n_tokens):
    t = jnp.arange(n_tokens)
    row = (t // GRID_W).astype(F32)
    col = (t % GRID_W).astype(F32)
    inv_freq = ROPE_BASE ** (-jnp.arange(ROPE_HALF, dtype=F32) / ROPE_HALF)
    ang_r = row[:, None] * inv_freq
    ang_c = col[:, None] * inv_freq
    cos_h = jnp.concatenate([jnp.cos(ang_r), jnp.cos(ang_r), jnp.cos(ang_c), jnp.cos(ang_c)], axis=-1)
    sin_h = jnp.concatenate([-jnp.sin(ang_r), jnp.sin(ang_r), -jnp.sin(ang_c), jnp.sin(ang_c)], axis=-1)
    reps = LANES // HEAD_DIM
    return jnp.tile(cos_h, (1, reps)), jnp.tile(sin_h, (1, reps))


def _pad_heads(dst_ref, rows, src):
    kv_dim = N_KV_HEADS * HEAD_DIM
    n = src.shape[0]
    lane = lax.broadcasted_iota(jnp.int32, (n, LANES), 1).astype(F32).astype(BF16)
    low = lane < HEAD_DIM
    zero = jnp.zeros((n, LANES), BF16)
    for t in range(2):
        for h in range(N_KV_HEADS):
            c = h // 2
            plain = src[:, 2 * t * kv_dim + c * LANES:2 * t * kv_dim + (c + 1) * LANES]
            swapped = src[:, (2 * t + 1) * kv_dim + c * LANES:(2 * t + 1) * kv_dim + (c + 1) * LANES]
            in_low, in_high = (plain, swapped) if h % 2 == 0 else (swapped, plain)
            base = t * 2 * N_KV_HEADS * LANES + h * 2 * LANES
            dst_ref[rows, base:base + LANES] = jnp.where(low, in_low, zero)
            dst_ref[rows, base + LANES:base + 2 * LANES] = jnp.where(low, zero, in_high)


def _lane_chunks(a):
    return [a[:, c * LANES:(c + 1) * LANES] for c in range(a.shape[1] // LANES)]


def _attn_kernel(*refs, tq, n_local_blocks, has_local, slot):
    if has_local:
        sink_ref, q_ref, kvo_ref, kvp_ref, kvn_ref, kvc_ref, o_ref, loc_pad, ctx_pad = refs
    else:
        sink_ref, q_ref, kvc_ref, o_ref, ctx_pad = refs
    blk = ATTN_BLOCK
    i = pl.program_id(1)
    n_ctx = kvc_ref.shape[1]
    v_base = 2 * N_KV_HEADS * LANES
    _pad_heads(ctx_pad, slice(0, n_ctx), kvc_ref[0])
    if has_local:
        _pad_heads(loc_pad, slice(0, blk), kvp_ref[0])
        _pad_heads(loc_pad, slice(blk, blk + tq), kvo_ref[0])
        _pad_heads(loc_pad, slice(blk + tq, 2 * blk + tq), kvn_ref[0])

    lane_low = lax.broadcasted_iota(jnp.int32, (2 * blk, LANES), 1) < HEAD_DIM
    row_first = lax.broadcasted_iota(jnp.int32, (2 * blk, 1), 0) < blk

    def q_block(j, carry):
        r0 = pl.multiple_of(j * blk, blk)
        if has_local:
            n = i * (tq // blk) + j
            qi = lax.broadcasted_iota(jnp.int32, (2 * blk, 3 * blk), 0) & (blk - 1)
            ki = lax.broadcasted_iota(jnp.int32, (2 * blk, 3 * blk), 1)
            rel = ki - blk - qi
            k_lo = jnp.where(n == 0, blk, 0)
            k_hi = jnp.where(n == n_local_blocks - 1, 2 * blk, 3 * blk)
            valid = (rel >= -blk) & (rel <= blk) & (ki >= k_lo) & (ki < k_hi)
        for h in range(N_KV_HEADS):
            qp = q_ref[0, pl.ds(r0, blk), h * 2 * LANES:(h + 1) * 2 * LANES]
            lhs = jnp.concatenate([qp[:, :LANES], qp[:, LANES:]], axis=0)
            kc = h * 2 * LANES
            vc = v_base + h * 2 * LANES
            nt = (((1,), (1,)), ((), ()))
            seg_scores = [[], []]
            k_ctx = jnp.concatenate([ctx_pad[:, kc:kc + LANES], ctx_pad[:, kc + LANES:kc + 2 * LANES]], axis=0)
            s_ctx = lax.dot_general(lhs, k_ctx, nt, preferred_element_type=F32)
            seg_scores[0].append(s_ctx[:, :n_ctx])
            seg_scores[1].append(s_ctx[:, n_ctx:])
            if has_local:
                k_loc = jnp.concatenate([loc_pad[pl.ds(r0, 3 * blk), kc:kc + LANES],
                                         loc_pad[pl.ds(r0, 3 * blk), kc + LANES:kc + 2 * LANES]], axis=0)
                s_loc = lax.dot_general(lhs, k_loc, nt, preferred_element_type=F32)
                seg_scores[0].append(jnp.where(valid, s_loc[:, :3 * blk], NEG_INF))
                seg_scores[1].append(jnp.where(valid, s_loc[:, 3 * blk:], NEG_INF))
            probs, inv = [], []
            for seg in range(2):
                sink = jnp.where(row_first, sink_ref[slot, h * GROUP + seg],
                                 sink_ref[slot, h * GROUP + 2 + seg]) * LOG2_E
                col_max = functools.reduce(jnp.maximum, [c for s in seg_scores[seg] for c in _lane_chunks(s)])
                m = jnp.maximum(sink, jnp.max(col_max, axis=-1, keepdims=True))
                ps = [jnp.exp2(s - m) for s in seg_scores[seg]]
                col_sum = functools.reduce(jnp.add, [c for p in ps for c in _lane_chunks(p)])
                denom = jnp.exp2(sink - m) + jnp.sum(col_sum, axis=-1, keepdims=True)
                probs.append(ps)
                inv.append(1.0 / denom)
            p_ctx = jnp.concatenate([probs[0][0], probs[1][0]], axis=1).astype(BF16)
            v_ctx = jnp.concatenate([ctx_pad[:, vc:vc + LANES], ctx_pad[:, vc + LANES:vc + 2 * LANES]], axis=0)
            o = _dot(p_ctx, v_ctx)
            if has_local:
                p_loc = jnp.concatenate([probs[0][1], probs[1][1]], axis=1).astype(BF16)
                v_loc = jnp.concatenate([loc_pad[pl.ds(r0, 3 * blk), vc:vc + LANES],
                                         loc_pad[pl.ds(r0, 3 * blk), vc + LANES:vc + 2 * LANES]], axis=0)
                o = o + _dot(p_loc, v_loc)
            o = o * jnp.where(lane_low, inv[0], inv[1])
            o_ref[0, pl.ds(r0, blk), h * 2 * LANES:h * 2 * LANES + LANES] = o[:blk].astype(BF16)
            o_ref[0, pl.ds(r0, blk), h * 2 * LANES + LANES:(h + 1) * 2 * LANES] = o[blk:].astype(BF16)
        return carry

    lax.fori_loop(0, tq // blk, q_block, 0)


def _attention(q, kv, kv_ctx, sink, slot, has_local):
    b, l, q_dim = q.shape
    n_ctx = kv_ctx.shape[1]
    blk = ATTN_BLOCK
    tq = min(1024, l)
    bpt = tq // blk
    n_blocks = l // blk
    pad_lanes = 4 * N_KV_HEADS * LANES
    in_specs = [pl.BlockSpec(memory_space=pltpu.SMEM),
                pl.BlockSpec((1, tq, q_dim), lambda bi, i: (bi, i, 0))]
    args = [sink, q]
    scratch = []
    if has_local:
        in_specs += [
            pl.BlockSpec((1, tq, q_dim), lambda bi, i: (bi, i, 0)),
            pl.BlockSpec((1, blk, q_dim), lambda bi, i: (bi, jnp.maximum(i * bpt - 1, 0), 0)),
            pl.BlockSpec((1, blk, q_dim), lambda bi, i: (bi, jnp.minimum((i + 1) * bpt, n_blocks - 1), 0)),
        ]
        args += [kv, kv, kv]
        scratch.append(pltpu.VMEM((tq + 2 * blk, pad_lanes), BF16))
    in_specs.append(pl.BlockSpec((1, n_ctx, q_dim), lambda bi, i: (bi, 0, 0)))
    args.append(kv_ctx)
    scratch.append(pltpu.VMEM((n_ctx, pad_lanes), BF16))
    return pl.pallas_call(
        functools.partial(_attn_kernel, tq=tq, n_local_blocks=n_blocks, has_local=has_local, slot=slot),
        grid=(b, l // tq),
        in_specs=in_specs,
        out_specs=pl.BlockSpec((1, tq, q_dim), lambda bi, i: (bi, i, 0)),
        out_shape=jax.ShapeDtypeStruct((b, l, q_dim), BF16),
        scratch_shapes=scratch,
        compiler_params=_cparams(2),
        name="window_attention" if has_local else "context_attention",
    )(*args)


def _oproj_ffn_kernel(a_ref, x_ref, g_ref, mod_ref, wo_ref, wgu_ref, wd_ref, o_ref):
    d_ff = wd_ref.shape[1]
    mod = lambda r: mod_ref[0, 0, r:r + 1, :]
    y = _dot(a_ref[0], wo_ref[0])
    x = x_ref[0] + mod(2) * _rmsnorm(y, g_ref[0, 1:2, :])
    h = _norm_mod(x, g_ref[0, 2:3, :], mod(3), mod(4)).astype(BF16)
    gate = _dot(h, wgu_ref[0, :, :d_ff])
    up = _dot(h, wgu_ref[0, :, d_ff:])
    f = _dot((_silu(gate) * up).astype(BF16), wd_ref[0])
    o_ref[0] = x + mod(5) * _rmsnorm(f, g_ref[0, 3:4, :])


def _oproj_ffn(attn, x, norm_g, mods, layer, mod_row, w_o, w_gu, w_down, slot):
    b, l, d = x.shape
    tm = min(512, l)
    resident = pl.Buffered(1)
    return pl.pallas_call(
        _oproj_ffn_kernel,
        grid=(b, l // tm),
        in_specs=[
            pl.BlockSpec((1, tm, attn.shape[2]), lambda bi, i: (bi, i, 0)),
            pl.BlockSpec((1, tm, d), lambda bi, i: (bi, i, 0)),
            _norm_spec(norm_g, layer, 2),
            _mod_spec(mods, layer, lambda bi, i: mod_row(bi)),
            _slot_spec(w_o, slot, pipeline_mode=resident),
            _slot_spec(w_gu, slot, pipeline_mode=resident),
            _slot_spec(w_down, slot, pipeline_mode=resident),
        ],
        out_specs=pl.BlockSpec((1, tm, d), lambda bi, i: (bi, i, 0)),
        out_shape=jax.ShapeDtypeStruct((b, l, d), F32),
        compiler_params=_cparams(2),
        name="oproj_ffn",
    )(attn, x, norm_g, mods, w_o, w_gu, w_down)


def _pool_kernel(x_ref, xp_ref, xn_ref, g_ref, mod_ref, pw_ref, pb_ref, ps_ref, rw_ref, rb_ref,
                 xo_ref, h2_ref, route_ref, hbuf, ybuf, *, tm, seq_len):
    i = pl.program_id(1)
    n_blk = pl.num_programs(1)
    sub = 128
    halo = POOL_HALO
    gd = pw_ref.shape[2]
    mod = lambda r: mod_ref[0, 0, r:r + 1, :]
    g_pre, shift, scale = g_ref[0, 0:1, :], mod(0), mod(1)
    x = x_ref[0]
    hbuf[0:halo, :] = _norm_mod(xp_ref[0], g_pre, shift, scale) * jnp.where(i > 0, 1.0, 0.0)
    hbuf[halo:halo + tm, :] = _norm_mod(x, g_pre, shift, scale)
    hbuf[halo + tm:, :] = _norm_mod(xn_ref[0], g_pre, shift, scale) * jnp.where(i < n_blk - 1, 1.0, 0.0)

    r = lax.broadcasted_iota(jnp.int32, (sub, 2 * sub), 0)
    c = lax.broadcasted_iota(jnp.int32, (sub, 2 * sub), 1)
    t_row = lax.broadcasted_iota(jnp.int32, (sub, 1), 0)
    fill = jnp.zeros((sub - 2 * halo, gd), BF16)
    for gi, w in enumerate(POOL_WINDOWS):
        band = jnp.where((c >= r + halo - w // 2) & (c < r + halo - w // 2 + w), 1.0, 0.0).astype(BF16)
        for s in range(tm // sub):
            hs = hbuf[s * sub:s * sub + sub + 2 * halo, gi * gd:(gi + 1) * gd]
            hi = hs.astype(BF16)
            lo = (hs - hi.astype(F32)).astype(BF16)
            total = (_dot(band, jnp.concatenate([hi, fill], axis=0))
                     + _dot(band, jnp.concatenate([lo, fill], axis=0)))
            t = i * tm + s * sub + t_row
            count = jnp.clip(t - w // 2 + w, 0, seq_len) - jnp.clip(t - w // 2, 0, seq_len)
            centre = hbuf[halo + s * sub:halo + (s + 1) * sub, gi * gd:(gi + 1) * gd]
            diff = total / count.astype(F32) - centre
            y = _dot(diff.astype(BF16), pw_ref[0, gi]) + pb_ref[0, :, gi * gd:(gi + 1) * gd]
            ybuf[s * sub:(s + 1) * sub, gi * gd:(gi + 1) * gd] = y * ps_ref[0, :, gi * gd:(gi + 1) * gd]

    xo = x + mod(2) * _rmsnorm(ybuf[...], g_ref[0, 1:2, :])
    xo_ref[0] = xo
    h2 = _norm_mod(xo, g_ref[0, 2:3, :], mod(3), mod(4))
    for s in range(SUBLANES):
        h2_ref[pl.ds(s, tm, stride=SUBLANES), :] = h2[:, s * LANES:(s + 1) * LANES]

    logits = _dot(h2.astype(BF16), rw_ref[0]) + rb_ref[0]
    lane = lax.broadcasted_iota(jnp.int32, logits.shape, 1)
    m1 = jnp.max(logits, axis=-1, keepdims=True)
    i1 = jnp.min(jnp.where(logits == m1, lane, LANES), axis=-1, keepdims=True)
    rest = jnp.where(lane == i1, NEG_INF * 2, logits)
    m2 = jnp.max(rest, axis=-1, keepdims=True)
    i2 = jnp.min(jnp.where(rest == m2, lane, LANES), axis=-1, keepdims=True)
    e = jnp.exp(m2 - m1)
    g1 = 1.0 / (1.0 + e)
    g2 = e / (1.0 + e)
    route_ref[0] = jnp.where(lane == 0, i1.astype(F32),
                             jnp.where(lane == 1, i2.astype(F32),
                                       jnp.where(lane == 2, g1, jnp.where(lane == 3, g2, 0.0))))


def _pool_mixer(x, norm_g, mods, layer, mod_row, pool_w, pool_b, pool_scale, router_w, router_b, slot):
    b, l, d = x.shape
    assert d == SUBLANES * LANES
    tm = min(512, l)
    hb = tm // POOL_HALO
    n_halo_blocks = l // POOL_HALO
    n_blk = l // tm
    row = lambda bi, i: (bi, i, 0)
    return pl.pallas_call(
        functools.partial(_pool_kernel, tm=tm, seq_len=l),
        grid=(b, n_blk),
        in_specs=[
            pl.BlockSpec((1, tm, d), row),
            pl.BlockSpec((1, POOL_HALO, d), lambda bi, i: (bi, jnp.maximum(i * hb - 1, 0), 0)),
            pl.BlockSpec((1, POOL_HALO, d), lambda bi, i: (bi, jnp.minimum((i + 1) * hb, n_halo_blocks - 1), 0)),
            _norm_spec(norm_g, layer, 2),
            _mod_spec(mods, layer, lambda bi, i: mod_row(bi)),
            _slot_spec(pool_w, slot),
            _slot_spec(pool_b, slot),
            _slot_spec(pool_scale, slot),
            _slot_spec(router_w, slot),
            _slot_spec(router_b, slot),
        ],
        out_specs=[pl.BlockSpec((1, tm, d), row),
                   pl.BlockSpec((tm * SUBLANES, LANES), lambda bi, i: (bi * n_blk + i, 0)),
                   pl.BlockSpec((1, tm, LANES), row)],
        out_shape=[jax.ShapeDtypeStruct((b, l, d), F32),
                   jax.ShapeDtypeStruct((b * l * SUBLANES, LANES), F32),
                   jax.ShapeDtypeStruct((b, l, LANES), F32)],
        scratch_shapes=[pltpu.VMEM((tm + 2 * POOL_HALO, d), F32), pltpu.VMEM((tm, d), F32)],
        compiler_params=_cparams(2),
        name="pool_mixer",
    )(x, x, x, norm_g, mods, pool_w, pool_b, pool_scale, router_w, router_b)


def _route_plan(route, tm):
    m = route.shape[0]
    n_tiles = (TOP_K * m) // tm + N_EXPERTS
    expert = route[:, :TOP_K].astype(jnp.int32).reshape(-1)
    onehot = (expert[:, None] == jnp.arange(N_EXPERTS, dtype=jnp.int32)[None, :]).astype(jnp.int32)
    csum = jnp.cumsum(onehot, axis=0)
    counts = csum[-1]
    rank = jnp.sum(csum * onehot, axis=1) - 1
    padded = ((counts + tm - 1) // tm) * tm
    ends = jnp.cumsum(padded)
    starts = ends - padded
    pos = jnp.sum(starts[None, :] * onehot, axis=1) + rank
    tile_start = jnp.arange(n_tiles, dtype=jnp.int32) * tm
    tile_expert = jnp.minimum(jnp.sum((tile_start[:, None] >= ends[None, :]).astype(jnp.int32), axis=1),
                              N_EXPERTS - 1)
    n_used = (ends[-1] // tm).reshape(1)
    i32 = lambda a: a.astype(jnp.int32)
    return i32(pos), i32(tile_expert), i32(n_used), i32(jnp.concatenate([starts, ends, n_used]))


def _tile(ref, row):
    return ref.at[pl.ds(pl.multiple_of(row * SUBLANES, SUBLANES), SUBLANES)]


def _wait_tiles(n, src_ref, dst_ref, sem):
    def wait(r, carry):
        pltpu.make_async_copy(_tile(src_ref, 0), _tile(dst_ref, 0), sem).wait()
        return carry
    lax.fori_loop(0, n, wait, 0, unroll=8)


def _dispatch_kernel(pos_ref, seg_ref, h_ref, xs_hbm, zeros_ref, sem, *, tc, tm):
    i = pl.program_id(0)
    zero_rows = zeros_ref.shape[0] // SUBLANES
    n_tiles = xs_hbm.shape[0] // (tm * SUBLANES)

    def clear_tile(first_row):
        for c in range(tm // zero_rows):
            first = pl.multiple_of((first_row + c * zero_rows) * SUBLANES, SUBLANES)
            cp = pltpu.make_async_copy(zeros_ref, xs_hbm.at[pl.ds(first, zero_rows * SUBLANES)], sem)
            cp.start()
            cp.wait()

    @pl.when(i == 0)
    def _():
        zeros_ref[...] = jnp.zeros_like(zeros_ref)
        for e in range(N_EXPERTS):
            start, end = seg_ref[e], seg_ref[N_EXPERTS + e]
            pl.when(end > start)(functools.partial(clear_tile, end - tm))
            spare = seg_ref[2 * N_EXPERTS] + e
            pl.when(spare < n_tiles)(functools.partial(clear_tile, spare * tm))

    base = i * tc * TOP_K

    def issue(r, carry):
        for k in range(TOP_K):
            p = pos_ref[base + TOP_K * r + k]
            pltpu.make_async_copy(_tile(h_ref, r), _tile(xs_hbm, p), sem).start(priority=k)
        return carry
    lax.fori_loop(0, tc, issue, 0, unroll=8)
    _wait_tiles(TOP_K * tc, h_ref, xs_hbm, sem)


def _dispatch(h2t, pos, seg, n_rows, tm):
    m = h2t.shape[0] // SUBLANES
    tc = min(1024, m)
    return pl.pallas_call(
        functools.partial(_dispatch_kernel, tc=tc, tm=tm),
        grid_spec=pltpu.PrefetchScalarGridSpec(
            num_scalar_prefetch=2,
            grid=(m // tc,),
            in_specs=[pl.BlockSpec((tc * SUBLANES, LANES), lambda i, pos, seg: (i, 0))],
            out_specs=pl.BlockSpec(memory_space=pl.ANY),
            scratch_shapes=[pltpu.VMEM((min(tm, 256) * SUBLANES, LANES), F32), pltpu.SemaphoreType.DMA],
        ),
        out_shape=jax.ShapeDtypeStruct((n_rows * SUBLANES, LANES), F32),
        compiler_params=_cparams(1),
        name="moe_dispatch",
    )(pos, seg, h2t)


def _experts_kernel(te_ref, nused_ref, xs_ref, wgu_ref, wd_ref, ys_ref, *, tm, n_chunks):
    i = pl.program_id(0)
    d_ff = wd_ref.shape[2]
    tf = d_ff // n_chunks

    @pl.when(i < nused_ref[0])
    def _():
        x = jnp.concatenate([xs_ref[pl.ds(s, tm, stride=SUBLANES), :].astype(BF16)
                             for s in range(SUBLANES)], axis=1)
        acc = None
        for c in range(n_chunks):
            gate = _dot(x, wgu_ref[0, 0, :, c * tf:(c + 1) * tf])
            up = _dot(x, wgu_ref[0, 0, :, d_ff + c * tf:d_ff + (c + 1) * tf])
            part = _dot((_silu(gate) * up).astype(BF16), wd_ref[0, 0, c * tf:(c + 1) * tf, :])
            acc = part if acc is None else acc + part
        for s in range(SUBLANES):
            ys_ref[pl.ds(s, tm, stride=SUBLANES), :] = acc[:, s * LANES:(s + 1) * LANES]

    @pl.when(i >= nused_ref[0])
    def _():
        ys_ref[...] = jnp.zeros_like(ys_ref)


def _experts(xs, tile_expert, n_used, w_gu, w_down, slot, tm):
    n_rows = xs.shape[0] // SUBLANES
    resident = pl.Buffered(1)
    return pl.pallas_call(
        functools.partial(_experts_kernel, tm=tm, n_chunks=2),
        grid_spec=pltpu.PrefetchScalarGridSpec(
            num_scalar_prefetch=2,
            grid=(n_rows // tm,),
            in_specs=[
                pl.BlockSpec((tm * SUBLANES, LANES), lambda i, te, nu: (i, 0)),
                pl.BlockSpec((1, 1) + w_gu.shape[2:], lambda i, te, nu: (slot, te[i], 0, 0), pipeline_mode=resident),
                pl.BlockSpec((1, 1) + w_down.shape[2:], lambda i, te, nu: (slot, te[i], 0, 0), pipeline_mode=resident),
            ],
            out_specs=pl.BlockSpec((tm * SUBLANES, LANES), lambda i, te, nu: (i, 0)),
        ),
        out_shape=jax.ShapeDtypeStruct((n_rows * SUBLANES, LANES), F32),
        compiler_params=_cparams(1),
        name="moe_experts",
    )(tile_expert, n_used, xs, w_gu, w_down)


def _combine_kernel(pos_ref, ys_hbm, route_ref, x_ref, g_ref, mod_ref, o_ref, buf, sem, *, tc):
    i = pl.program_id(0)
    n = pl.num_programs(0)

    def gather(block, slot):
        base = block * tc * TOP_K

        def issue(r, carry):
            for k in range(TOP_K):
                p = pos_ref[base + TOP_K * r + k]
                pltpu.make_async_copy(_tile(ys_hbm, p), _tile(buf.at[slot, k], r), sem.at[slot]).start(priority=k)
            return carry
        lax.fori_loop(0, tc, issue, 0, unroll=8)

    @pl.when(i == 0)
    def _():
        gather(0, 0)

    @pl.when(i + 1 < n)
    def _():
        gather(i + 1, (i + 1) % 2)

    slot = i % 2
    _wait_tiles(TOP_K * tc, ys_hbm, buf.at[slot, 0], sem.at[slot])
    rows = lambda k: jnp.concatenate([buf[slot, k, pl.ds(s, tc, stride=SUBLANES), :] for s in range(SUBLANES)], axis=1)
    route = route_ref[...]
    f = route[:, 2:3] * rows(0) + route[:, 3:4] * rows(1)
    o_ref[...] = x_ref[...] + mod_ref[0, 0, 5:6, :] * _rmsnorm(f, g_ref[0, 3:4, :])


def _combine(ys, pos, route, x, norm_g, mods, layer, mod_row_of_block, tc):
    m, d = x.shape
    return pl.pallas_call(
        functools.partial(_combine_kernel, tc=tc),
        grid_spec=pltpu.PrefetchScalarGridSpec(
            num_scalar_prefetch=1,
            grid=(m // tc,),
            in_specs=[
                pl.BlockSpec(memory_space=pl.ANY),
                pl.BlockSpec((tc, LANES), lambda i, pos: (i, 0)),
                pl.BlockSpec((tc, d), lambda i, pos: (i, 0)),
                _norm_spec(norm_g, layer, 1),
                _mod_spec(mods, layer, lambda i, pos: mod_row_of_block(i)),
            ],
            out_specs=pl.BlockSpec((tc, d), lambda i, pos: (i, 0)),
            scratch_shapes=[pltpu.VMEM((2, TOP_K, tc * SUBLANES, LANES), F32),
                            pltpu.SemaphoreType.DMA((2,))],
        ),
        out_shape=jax.ShapeDtypeStruct((m, d), F32),
        compiler_params=_cparams(1),
        name="moe_combine",
    )(pos, ys, route, x, norm_g, mods)


def _moe(h2t, route, x, norm_g, mods, layer, mod_row, w_gu, w_down, slot):
    b, l, d = x.shape
    m = b * l
    tm = min(512, l)
    tc = min(512, l)
    route2 = route.reshape(m, LANES)
    pos, tile_expert, n_used, seg = _route_plan(route2, tm)
    n_rows = TOP_K * m + N_EXPERTS * tm
    xs = _dispatch(h2t, pos, seg, n_rows, tm)
    ys = _experts(xs, tile_expert, n_used, w_gu, w_down, slot, tm)
    blocks_per_seq = l // tc
    out = _combine(ys, pos, route2, x.reshape(m, d), norm_g, mods, layer,
                   lambda i: mod_row(i // blocks_per_seq), tc)
    return out.reshape(b, l, d)


def kernel(x, c, ctx, c_ctx, ada_w, ada_b, norm_g, attn_w_qkv, attn_w_o, attn_sink, pool_w, pool_b,
           pool_scale, ffn_w_gu, ffn_w_down, router_w, router_b, moe_w_gu, moe_w_down):
    batch, seq, d = x.shape
    depth = ada_w.shape[0]
    ctx_row = batch
    assert batch < MOD_ROWS and seq % ATTN_BLOCK == 0 and d == N_HEADS * HEAD_DIM

    cond = jnp.zeros((MOD_ROWS, d), F32).at[:batch].set(c).at[ctx_row].set(c_ctx)
    mods = _ada_mods(cond, ada_w, ada_b).reshape(depth, MOD_ROWS, 6, d)
    latent_row = lambda bi: bi
    context_row = lambda bi: ctx_row
    rope = _rope_tables(seq)

    w_qkv = attn_w_qkv.astype(BF16)
    w_o = attn_w_o.astype(BF16)
    w_gu = ffn_w_gu.astype(BF16)
    w_down = ffn_w_down.astype(BF16)
    p_w = pool_w.astype(BF16)
    p_b = pool_b.reshape(-1, 1, d)
    p_s = pool_scale.reshape(-1, 1, d)
    e_gu = moe_w_gu.astype(BF16)
    e_down = moe_w_down.astype(BF16)
    n_exp = router_w.shape[2]
    r_w = jnp.zeros((router_w.shape[0], d, LANES), BF16).at[:, :, :n_exp].set(router_w.astype(BF16))
    r_b = jnp.full((router_b.shape[0], 1, LANES), NEG_INF, F32).at[:, 0, :n_exp].set(router_b)

    is_attn = [i % 2 == 0 for i in range(depth)]
    for i in range(depth):
        slot = i // 2
        ctx_later = any(is_attn[i + 1:])
        if is_attn[i]:
            q, kv = _qkv_proj(x, norm_g, mods, i, latent_row, w_qkv, slot, rope)
            qc, kvc = _qkv_proj(ctx, norm_g, mods, i, context_row, w_qkv, slot, None)
            attn = _attention(q, kv, kvc, attn_sink, slot, True)
            x = _oproj_ffn(attn, x, norm_g, mods, i, latent_row, w_o, w_gu, w_down, slot)
            if ctx_later:
                attn_c = _attention(qc, None, kvc, attn_sink, slot, False)
                ctx = _oproj_ffn(attn_c, ctx, norm_g, mods, i, context_row, w_o, w_gu, w_down, slot)
        else:
            x1, h2t, route = _pool_mixer(x, norm_g, mods, i, latent_row, p_w, p_b, p_s, r_w, r_b, slot)
            x = _moe(h2t, route, x1, norm_g, mods, i, latent_row, e_gu, e_down, slot)
            if ctx_later:
                c1, h2c, route_c = _pool_mixer(ctx, norm_g, mods, i, context_row, p_w, p_b, p_s, r_w, r_b, slot)
                ctx = _moe(h2c, route_c, c1, norm_g, mods, i, context_row, e_gu, e_down, slot)
    return x
```

```python
import functools

import jax
import jax.numpy as jnp
from jax import lax
from jax.experimental import pallas as pl
from jax.experimental.pallas import tpu as pltpu

F32 = jnp.float32
BF16 = jnp.bfloat16

N_HEADS = 16
N_KV_HEADS = 4
HEAD_DIM = 64
GROUP = N_HEADS // N_KV_HEADS
ATTN_BLOCK = 128
GRID_W = 64
ROPE_BASE = 10000.0
ROPE_HALF = HEAD_DIM // 4
POOL_WINDOWS = (2, 4, 8, 16)
POOL_HALO = 8
N_EXPERTS = 8
TOP_K = 2
NORM_EPS = 1e-6
NEG_INF = -1e30
LOG2_E = 1.4426950408889634
Q_SCALE = HEAD_DIM ** -0.5 * LOG2_E
LANES = 128
SUBLANES = 8
MOD_ROWS = 8
VMEM_LIMIT = 56 * 1024 * 1024


def _cparams(n_axes):
    return pltpu.CompilerParams(dimension_semantics=("arbitrary",) * n_axes,
                                vmem_limit_bytes=VMEM_LIMIT)


def _rmsnorm(x, g):
    return x * lax.rsqrt(jnp.mean(x * x, axis=-1, keepdims=True) + NORM_EPS) * g


def _norm_mod(x, g, shift, scale):
    return _rmsnorm(x, g) * (1.0 + scale) + shift


def _silu(x):
    return x / (1.0 + jnp.exp(-x))


def _dot(a, b):
    return jnp.dot(a, b, preferred_element_type=F32)


def _norm_spec(norm_g, layer, n_grid):
    return pl.BlockSpec((1,) + norm_g.shape[1:], lambda *_: (layer, 0, 0))


def _mod_spec(mods, layer, row_of):
    return pl.BlockSpec((1, 1) + mods.shape[2:], lambda *idx: (layer, row_of(*idx), 0, 0))


def _slot_spec(w, slot, **kw):
    zeros = (0,) * (w.ndim - 1)
    return pl.BlockSpec((1,) + w.shape[1:], lambda *_: (slot,) + zeros, **kw)


def _ada_kernel(cond_ref, w_ref, b_ref, o_ref):
    s = _silu(cond_ref[...]).astype(BF16)
    o_ref[0] = _dot(s, w_ref[0].astype(BF16)) + b_ref[0]


def _ada_mods(cond, ada_w, ada_b):
    depth, d, n = ada_w.shape
    tn = 1536
    return pl.pallas_call(
        _ada_kernel,
        grid=(depth, n // tn),
        in_specs=[
            pl.BlockSpec((MOD_ROWS, d), lambda l, j: (0, 0)),
            pl.BlockSpec((1, d, tn), lambda l, j: (l, 0, j)),
            pl.BlockSpec((1, 1, tn), lambda l, j: (l, 0, j)),
        ],
        out_specs=pl.BlockSpec((1, MOD_ROWS, tn), lambda l, j: (l, 0, j)),
        out_shape=jax.ShapeDtypeStruct((depth, MOD_ROWS, n), F32),
        compiler_params=_cparams(2),
        name="ada_mods",
    )(cond, ada_w, ada_b.reshape(depth, 1, n))


def _rope(chunk, cos, sin, take_upper):
    rot = jnp.where(take_upper, pltpu.roll(chunk, LANES - ROPE_HALF, axis=1),
                    pltpu.roll(chunk, ROPE_HALF, axis=1))
    return chunk * cos + rot * sin


def _qkv_kernel(*refs, rope):
    if rope:
        x_ref, g_ref, mod_ref, w_ref, cos_ref, sin_ref, q_ref, kv_ref = refs
    else:
        x_ref, g_ref, mod_ref, w_ref, q_ref, kv_ref = refs
    h = _norm_mod(x_ref[0], g_ref[0, 0:1, :], mod_ref[0, 0, 0:1, :], mod_ref[0, 0, 1:2, :])
    y = _dot(h.astype(BF16), w_ref[0])
    q_dim = N_HEADS * HEAD_DIM
    kv_dim = N_KV_HEADS * HEAD_DIM
    if rope:
        cos, sin = cos_ref[...], sin_ref[...]
        lane = lax.broadcasted_iota(jnp.int32, cos.shape, 1)
        take_upper = (lane % (2 * ROPE_HALF)) < ROPE_HALF
    for c in range(q_dim // LANES):
        chunk = y[:, c * LANES:(c + 1) * LANES]
        if rope:
            chunk = _rope(chunk, cos, sin, take_upper)
        q_ref[0, :, c * LANES:(c + 1) * LANES] = (chunk * Q_SCALE).astype(BF16)
    for c in range(kv_dim // LANES):
        k = y[:, q_dim + c * LANES:q_dim + (c + 1) * LANES]
        if rope:
            k = _rope(k, cos, sin, take_upper)
        v = y[:, q_dim + kv_dim + c * LANES:q_dim + kv_dim + (c + 1) * LANES]
        kv_ref[0, :, c * LANES:(c + 1) * LANES] = k.astype(BF16)
        kv_ref[0, :, kv_dim + c * LANES:kv_dim + (c + 1) * LANES] = pltpu.roll(k, HEAD_DIM, axis=1).astype(BF16)
        kv_ref[0, :, 2 * kv_dim + c * LANES:2 * kv_dim + (c + 1) * LANES] = v.astype(BF16)
        kv_ref[0, :, 3 * kv_dim + c * LANES:3 * kv_dim + (c + 1) * LANES] = pltpu.roll(v, HEAD_DIM, axis=1).astype(BF16)


def _qkv_proj(x, norm_g, mods, layer, mod_row, w_qkv, slot, rope_tables):
    b, l, d = x.shape
    tm = min(512, l)
    rope = rope_tables is not None
    in_specs = [
        pl.BlockSpec((1, tm, d), lambda bi, i: (bi, i, 0)),
        _norm_spec(norm_g, layer, 2),
        _mod_spec(mods, layer, lambda bi, i: mod_row(bi)),
        _slot_spec(w_qkv, slot),
    ]
    args = [x, norm_g, mods, w_qkv]
    if rope:
        in_specs += [pl.BlockSpec((tm, LANES), lambda bi, i: (i, 0))] * 2
        args += list(rope_tables)
    q_dim = N_HEADS * HEAD_DIM
    return pl.pallas_call(
        functools.partial(_qkv_kernel, rope=rope),
        grid=(b, l // tm),
        in_specs=in_specs,
        out_specs=[pl.BlockSpec((1, tm, q_dim), lambda bi, i: (bi, i, 0)),
                   pl.BlockSpec((1, tm, q_dim), lambda bi, i: (bi, i, 0))],
        out_shape=[jax.ShapeDtypeStruct((b, l, q_dim), BF16),
                   jax.ShapeDtypeStruct((b, l, q_dim), BF16)],
        compiler_params=_cparams(2),
        name="qkv_rope" if rope else "qkv",
    )(*args)


def _rope_tables(n_tokens):
    t = jnp.arange(n_tokens)
    row = (t // GRID_W).astype(F32)
    col = (t % GRID_W).astype(F32)
    inv_freq = ROPE_BASE ** (-jnp.arange(ROPE_HALF, dtype=F32) / ROPE_HALF)
    ang_r = row[:, None] * inv_freq
    ang_c = col[:, None] * inv_freq
    cos_h = jnp.concatenate([jnp.cos(ang_r), jnp.cos(ang_r), jnp.cos(ang_c), jnp.cos(ang_c)], axis=-1)
    sin_h = jnp.concatenate([-jnp.sin(ang_r), jnp.sin(ang_r), -jnp.sin(ang_c), jnp.sin(ang_c)], axis=-1)
    reps = LANES // HEAD_DIM
    return jnp.tile(cos_h, (1, reps)), jnp.tile(sin_h, (1, reps))


def _pad_heads(dst_ref, rows, src):
    kv_dim = N_KV_HEADS * HEAD_DIM
    n = src.shape[0]
    lane = lax.broadcasted_iota(jnp.int32, (n, LANES), 1).astype(F32).astype(BF16)
    low = lane < HEAD_DIM
    zero = jnp.zeros((n, LANES), BF16)
    for t in range(2):
        for h in range(N_KV_HEADS):
            c = h // 2
            plain = src[:, 2 * t * kv_dim + c * LANES:2 * t * kv_dim + (c + 1) * LANES]
            swapped = src[:, (2 * t + 1) * kv_dim + c * LANES:(2 * t + 1) * kv_dim + (c + 1) * LANES]
            in_low, in_high = (plain, swapped) if h % 2 == 0 else (swapped, plain)
            base = t * 2 * N_KV_HEADS * LANES + h * 2 * LANES
            dst_ref[rows, base:base + LANES] = jnp.where(low, in_low, zero)
            dst_ref[rows, base + LANES:base + 2 * LANES] = jnp.where(low, zero, in_high)


def _lane_chunks(a):
    return [a[:, c * LANES:(c + 1) * LANES] for c in range(a.shape[1] // LANES)]


def _cast_spec(w, slot, n_steps, step_of):
    rows = w.shape[1] // n_steps
    assert rows * n_steps == w.shape[1] and rows % 16 == 0
    src = pl.BlockSpec((1, rows, w.shape[2]), lambda *idx: (slot, step_of(*idx), 0))
    dst = pl.BlockSpec((rows, w.shape[2]), lambda *idx: (step_of(*idx), 0))
    return src, dst, jax.ShapeDtypeStruct(w.shape[1:], BF16)


def _attn_kernel(*refs, tq, n_local_blocks, has_local, slot, side_cast):
    n_in = (6 if has_local else 3) + (1 if side_cast else 0)
    ins, rest = refs[:n_in], list(refs[n_in:])
    if has_local:
        sink_ref, q_ref, kvo_ref, kvp_ref, kvn_ref, kvc_ref = ins[:6]
    else:
        sink_ref, q_ref, kvc_ref = ins[:3]
    o_ref = rest.pop(0)
    if side_cast:
        rest.pop(0)[...] = ins[-1][0].astype(BF16)
    if has_local:
        loc_pad, ctx_pad = rest
    else:
        (ctx_pad,) = rest
    blk = ATTN_BLOCK
    i = pl.program_id(1)
    n_ctx = kvc_ref.shape[1]
    v_base = 2 * N_KV_HEADS * LANES
    _pad_heads(ctx_pad, slice(0, n_ctx), kvc_ref[0])
    if has_local:
        _pad_heads(loc_pad, slice(0, blk), kvp_ref[0])
        _pad_heads(loc_pad, slice(blk, blk + tq), kvo_ref[0])
        _pad_heads(loc_pad, slice(blk + tq, 2 * blk + tq), kvn_ref[0])

    lane_low = lax.broadcasted_iota(jnp.int32, (2 * blk, LANES), 1) < HEAD_DIM
    row_first = lax.broadcasted_iota(jnp.int32, (2 * blk, 1), 0) < blk

    def q_block(j, carry):
        r0 = pl.multiple_of(j * blk, blk)
        if has_local:
            n = i * (tq // blk) + j
            qi = lax.broadcasted_iota(jnp.int32, (2 * blk, 3 * blk), 0) & (blk - 1)
            ki = lax.broadcasted_iota(jnp.int32, (2 * blk, 3 * blk), 1)
            rel = ki - blk - qi
            k_lo = jnp.where(n == 0, blk, 0)
            k_hi = jnp.where(n == n_local_blocks - 1, 2 * blk, 3 * blk)
            valid = (rel >= -blk) & (rel <= blk) & (ki >= k_lo) & (ki < k_hi)
        for h in range(N_KV_HEADS):
            qp = q_ref[0, pl.ds(r0, blk), h * 2 * LANES:(h + 1) * 2 * LANES]
            lhs = jnp.concatenate([qp[:, :LANES], qp[:, LANES:]], axis=0)
            kc = h * 2 * LANES
            vc = v_base + h * 2 * LANES
            nt = (((1,), (1,)), ((), ()))
            seg_scores = [[], []]
            k_ctx = jnp.concatenate([ctx_pad[:, kc:kc + LANES], ctx_pad[:, kc + LANES:kc + 2 * LANES]], axis=0)
            s_ctx = lax.dot_general(lhs, k_ctx, nt, preferred_element_type=F32)
            seg_scores[0].append(s_ctx[:, :n_ctx])
            seg_scores[1].append(s_ctx[:, n_ctx:])
            if has_local:
                k_loc = jnp.concatenate([loc_pad[pl.ds(r0, 3 * blk), kc:kc + LANES],
                                         loc_pad[pl.ds(r0, 3 * blk), kc + LANES:kc + 2 * LANES]], axis=0)
                s_loc = lax.dot_general(lhs, k_loc, nt, preferred_element_type=F32)
                seg_scores[0].append(jnp.where(valid, s_loc[:, :3 * blk], NEG_INF))
                seg_scores[1].append(jnp.where(valid, s_loc[:, 3 * blk:], NEG_INF))
            probs, inv = [], []
            for seg in range(2):
                sink = jnp.where(row_first, sink_ref[slot, h * GROUP + seg],
                                 sink_ref[slot, h * GROUP + 2 + seg]) * LOG2_E
                col_max = functools.reduce(jnp.maximum, [c for s in seg_scores[seg] for c in _lane_chunks(s)])
                m = jnp.maximum(sink, jnp.max(col_max, axis=-1, keepdims=True))
                ps = [jnp.exp2(s - m) for s in seg_scores[seg]]
                col_sum = functools.reduce(jnp.add, [c for p in ps for c in _lane_chunks(p)])
                denom = jnp.exp2(sink - m) + jnp.sum(col_sum, axis=-1, keepdims=True)
                probs.append(ps)
                inv.append(1.0 / denom)
            p_ctx = jnp.concatenate([probs[0][0], probs[1][0]], axis=1).astype(BF16)
            v_ctx = jnp.concatenate([ctx_pad[:, vc:vc + LANES], ctx_pad[:, vc + LANES:vc + 2 * LANES]], axis=0)
            o = _dot(p_ctx, v_ctx)
            if has_local:
                p_loc = jnp.concatenate([probs[0][1], probs[1][1]], axis=1).astype(BF16)
                v_loc = jnp.concatenate([loc_pad[pl.ds(r0, 3 * blk), vc:vc + LANES],
                                         loc_pad[pl.ds(r0, 3 * blk), vc + LANES:vc + 2 * LANES]], axis=0)
                o = o + _dot(p_loc, v_loc)
            o = o * jnp.where(lane_low, inv[0], inv[1])
            o_ref[0, pl.ds(r0, blk), h * 2 * LANES:h * 2 * LANES + LANES] = o[:blk].astype(BF16)
            o_ref[0, pl.ds(r0, blk), h * 2 * LANES + LANES:(h + 1) * 2 * LANES] = o[blk:].astype(BF16)
        return carry

    lax.fori_loop(0, tq // blk, q_block, 0, unroll=2)


def _attention(q, kv, kv_ctx, sink, slot, has_local, cast=None):
    b, l, q_dim = q.shape
    n_ctx = kv_ctx.shape[1]
    blk = ATTN_BLOCK
    tq = min(1024, l)
    bpt = tq // blk
    n_blocks = l // blk
    pad_lanes = 4 * N_KV_HEADS * LANES
    in_specs = [pl.BlockSpec(memory_space=pltpu.SMEM),
                pl.BlockSpec((1, tq, q_dim), lambda bi, i: (bi, i, 0))]
    args = [sink, q]
    scratch = []
    if has_local:
        in_specs += [
            pl.BlockSpec((1, tq, q_dim), lambda bi, i: (bi, i, 0)),
            pl.BlockSpec((1, blk, q_dim), lambda bi, i: (bi, jnp.maximum(i * bpt - 1, 0), 0)),
            pl.BlockSpec((1, blk, q_dim), lambda bi, i: (bi, jnp.minimum((i + 1) * bpt, n_blocks - 1), 0)),
        ]
        args += [kv, kv, kv]
        scratch.append(pltpu.VMEM((tq + 2 * blk, pad_lanes), BF16))
    in_specs.append(pl.BlockSpec((1, n_ctx, q_dim), lambda bi, i: (bi, 0, 0)))
    args.append(kv_ctx)
    scratch.append(pltpu.VMEM((n_ctx, pad_lanes), BF16))
    out_specs = [pl.BlockSpec((1, tq, q_dim), lambda bi, i: (bi, i, 0))]
    out_shape = [jax.ShapeDtypeStruct((b, l, q_dim), BF16)]
    if cast is not None:
        n_i = l // tq
        src, dst, shape = _cast_spec(cast[0], cast[1], b * n_i, lambda bi, i: bi * n_i + i)
        in_specs.append(src)
        args.append(cast[0])
        out_specs.append(dst)
        out_shape.append(shape)
    out = pl.pallas_call(
        functools.partial(_attn_kernel, tq=tq, n_local_blocks=n_blocks, has_local=has_local, slot=slot,
                          side_cast=cast is not None),
        grid=(b, l // tq),
        in_specs=in_specs,
        out_specs=out_specs,
        out_shape=out_shape,
        scratch_shapes=scratch,
        compiler_params=_cparams(2),
        name="window_attention" if has_local else "context_attention",
    )(*args)
    return out if cast is not None else out[0]


def _oproj_ffn_kernel(a_ref, x_ref, g_ref, mod_ref, wo_ref, wgu_ref, wd_ref, *rest):
    if len(rest) == 3:
        cast_src, o_ref, cast_dst = rest
        cast_dst[...] = cast_src[0].astype(BF16)
    else:
        (o_ref,) = rest
    d_ff = wd_ref.shape[1]
    mod = lambda r: mod_ref[0, 0, r:r + 1, :]
    y = _dot(a_ref[0], wo_ref[0])
    x = x_ref[0] + mod(2) * _rmsnorm(y, g_ref[0, 1:2, :])
    h = _norm_mod(x, g_ref[0, 2:3, :], mod(3), mod(4)).astype(BF16)
    gate = _dot(h, wgu_ref[0, :, :d_ff])
    up = _dot(h, wgu_ref[0, :, d_ff:])
    f = _dot((_silu(gate) * up).astype(BF16), wd_ref[0])
    o_ref[0] = x + mod(5) * _rmsnorm(f, g_ref[0, 3:4, :])


def _oproj_ffn(attn, x, norm_g, mods, layer, mod_row, w_o, w_gu, w_down, slot, cast=None):
    b, l, d = x.shape
    tm = min(512, l)
    resident = pl.Buffered(1)
    in_specs = [
        pl.BlockSpec((1, tm, attn.shape[2]), lambda bi, i: (bi, i, 0)),
        pl.BlockSpec((1, tm, d), lambda bi, i: (bi, i, 0)),
        _norm_spec(norm_g, layer, 2),
        _mod_spec(mods, layer, lambda bi, i: mod_row(bi)),
        _slot_spec(w_o, slot, pipeline_mode=resident),
        _slot_spec(w_gu, slot, pipeline_mode=resident),
        _slot_spec(w_down, slot, pipeline_mode=resident),
    ]
    args = [attn, x, norm_g, mods, w_o, w_gu, w_down]
    out_specs = [pl.BlockSpec((1, tm, d), lambda bi, i: (bi, i, 0))]
    out_shape = [jax.ShapeDtypeStruct((b, l, d), F32)]
    if cast is not None:
        n_i = l // tm
        src, dst, shape = _cast_spec(cast[0], cast[1], b * n_i, lambda bi, i: bi * n_i + i)
        in_specs.append(src)
        args.append(cast[0])
        out_specs.append(dst)
        out_shape.append(shape)
    out = pl.pallas_call(
        _oproj_ffn_kernel,
        grid=(b, l // tm),
        in_specs=in_specs,
        out_specs=out_specs,
        out_shape=out_shape,
        compiler_params=_cparams(2),
        name="oproj_ffn",
    )(*args)
    return out if cast is not None else out[0]


def _pool_kernel(x_ref, xp_ref, xn_ref, g_ref, mod_ref, pw_ref, pb_ref, ps_ref, rw_ref, rb_ref,
                 xo_ref, h2_ref, route_ref, hbuf, ybuf, *, tm, seq_len):
    i = pl.program_id(1)
    n_blk = pl.num_programs(1)
    sub = 128
    halo = POOL_HALO
    gd = pw_ref.shape[2]
    mod = lambda r: mod_ref[0, 0, r:r + 1, :]
    g_pre, shift, scale = g_ref[0, 0:1, :], mod(0), mod(1)
    x = x_ref[0]
    hbuf[0:halo, :] = _norm_mod(xp_ref[0], g_pre, shift, scale) * jnp.where(i > 0, 1.0, 0.0)
    hbuf[halo:halo + tm, :] = _norm_mod(x, g_pre, shift, scale)
    hbuf[halo + tm:, :] = _norm_mod(xn_ref[0], g_pre, shift, scale) * jnp.where(i < n_blk - 1, 1.0, 0.0)

    r = lax.broadcasted_iota(jnp.int32, (sub, 2 * sub), 0)
    c = lax.broadcasted_iota(jnp.int32, (sub, 2 * sub), 1)
    t_row = lax.broadcasted_iota(jnp.int32, (sub, 1), 0)
    fill = jnp.zeros((sub - 2 * halo, gd), BF16)
    for gi, w in enumerate(POOL_WINDOWS):
        band = jnp.where((c >= r + halo - w // 2) & (c < r + halo - w // 2 + w), 1.0, 0.0).astype(BF16)
        for s in range(tm // sub):
            hs = hbuf[s * sub:s * sub + sub + 2 * halo, gi * gd:(gi + 1) * gd]
            hi = hs.astype(BF16)
            lo = (hs - hi.astype(F32)).astype(BF16)
            total = (_dot(band, jnp.concatenate([hi, fill], axis=0))
                     + _dot(band, jnp.concatenate([lo, fill], axis=0)))
            t = i * tm + s * sub + t_row
            count = jnp.clip(t - w // 2 + w, 0, seq_len) - jnp.clip(t - w // 2, 0, seq_len)
            centre = hbuf[halo + s * sub:halo + (s + 1) * sub, gi * gd:(gi + 1) * gd]
            diff = total * (1.0 / count.astype(F32)) - centre
            y = _dot(diff.astype(BF16), pw_ref[0, gi]) + pb_ref[0, :, gi * gd:(gi + 1) * gd]
            ybuf[s * sub:(s + 1) * sub, gi * gd:(gi + 1) * gd] = y * ps_ref[0, :, gi * gd:(gi + 1) * gd]

    xo = x + mod(2) * _rmsnorm(ybuf[...], g_ref[0, 1:2, :])
    xo_ref[0] = xo
    h2 = _norm_mod(xo, g_ref[0, 2:3, :], mod(3), mod(4))
    for s in range(SUBLANES):
        h2_ref[pl.ds(s, tm, stride=SUBLANES), :] = h2[:, s * LANES:(s + 1) * LANES]

    logits = _dot(h2.astype(BF16), rw_ref[0]) + rb_ref[0]
    lane = lax.broadcasted_iota(jnp.int32, logits.shape, 1)
    m1 = jnp.max(logits, axis=-1, keepdims=True)
    i1 = jnp.min(jnp.where(logits == m1, lane, LANES), axis=-1, keepdims=True)
    rest = jnp.where(lane == i1, NEG_INF * 2, logits)
    m2 = jnp.max(rest, axis=-1, keepdims=True)
    i2 = jnp.min(jnp.where(rest == m2, lane, LANES), axis=-1, keepdims=True)
    e = jnp.exp(m2 - m1)
    g1 = 1.0 / (1.0 + e)
    g2 = e / (1.0 + e)
    route_ref[0] = jnp.where(lane == 0, i1.astype(F32),
                             jnp.where(lane == 1, i2.astype(F32),
                                       jnp.where(lane == 2, g1, jnp.where(lane == 3, g2, 0.0))))


def _pool_mixer(x, norm_g, mods, layer, mod_row, pool_w, pool_b, pool_scale, router_w, router_b, slot):
    b, l, d = x.shape
    assert d == SUBLANES * LANES
    tm = min(512, l)
    hb = tm // POOL_HALO
    n_halo_blocks = l // POOL_HALO
    n_blk = l // tm
    row = lambda bi, i: (bi, i, 0)
    return pl.pallas_call(
        functools.partial(_pool_kernel, tm=tm, seq_len=l),
        grid=(b, n_blk),
        in_specs=[
            pl.BlockSpec((1, tm, d), row),
            pl.BlockSpec((1, POOL_HALO, d), lambda bi, i: (bi, jnp.maximum(i * hb - 1, 0), 0)),
            pl.BlockSpec((1, POOL_HALO, d), lambda bi, i: (bi, jnp.minimum((i + 1) * hb, n_halo_blocks - 1), 0)),
            _norm_spec(norm_g, layer, 2),
            _mod_spec(mods, layer, lambda bi, i: mod_row(bi)),
            _slot_spec(pool_w, slot),
            _slot_spec(pool_b, slot),
            _slot_spec(pool_scale, slot),
            _slot_spec(router_w, slot),
            _slot_spec(router_b, slot),
        ],
        out_specs=[pl.BlockSpec((1, tm, d), row),
                   pl.BlockSpec((tm * SUBLANES, LANES), lambda bi, i: (bi * n_blk + i, 0)),
                   pl.BlockSpec((1, tm, LANES), row)],
        out_shape=[jax.ShapeDtypeStruct((b, l, d), F32),
                   jax.ShapeDtypeStruct((b * l * SUBLANES, LANES), F32),
                   jax.ShapeDtypeStruct((b, l, LANES), F32)],
        scratch_shapes=[pltpu.VMEM((tm + 2 * POOL_HALO, d), F32), pltpu.VMEM((tm, d), F32)],
        compiler_params=_cparams(2),
        name="pool_mixer",
    )(x, x, x, norm_g, mods, pool_w, pool_b, pool_scale, router_w, router_b)


def _route_plan(route, tm):
    m = route.shape[0]
    n_tiles = (TOP_K * m) // tm + N_EXPERTS
    expert = route[:, :TOP_K].astype(jnp.int32).reshape(-1)
    onehot = (expert[:, None] == jnp.arange(N_EXPERTS, dtype=jnp.int32)[None, :]).astype(jnp.int32)
    csum = jnp.cumsum(onehot, axis=0)
    counts = csum[-1]
    rank = jnp.sum(csum * onehot, axis=1) - 1
    padded = ((counts + tm - 1) // tm) * tm
    ends = jnp.cumsum(padded)
    starts = ends - padded
    pos = jnp.sum(starts[None, :] * onehot, axis=1) + rank
    tile_start = jnp.arange(n_tiles, dtype=jnp.int32) * tm
    tile_expert = jnp.minimum(jnp.sum((tile_start[:, None] >= ends[None, :]).astype(jnp.int32), axis=1),
                              N_EXPERTS - 1)
    n_used = (ends[-1] // tm).reshape(1)
    i32 = lambda a: a.astype(jnp.int32)
    return i32(pos), i32(tile_expert), i32(n_used), i32(jnp.concatenate([starts, ends, n_used]))


def _tile(ref, row):
    return ref.at[pl.ds(pl.multiple_of(row * SUBLANES, SUBLANES), SUBLANES)]


def _wait_tiles(n, src_ref, dst_ref, sem):
    def wait(r, carry):
        pltpu.make_async_copy(_tile(src_ref, 0), _tile(dst_ref, 0), sem).wait()
        return carry
    lax.fori_loop(0, n, wait, 0, unroll=8)


def _dispatch_kernel(pos_ref, seg_ref, h_ref, xs_hbm, zeros_ref, sem, *, tc, tm):
    i = pl.program_id(0)
    zero_rows = zeros_ref.shape[0] // SUBLANES
    n_tiles = xs_hbm.shape[0] // (tm * SUBLANES)

    def clear_tile(first_row):
        for c in range(tm // zero_rows):
            first = pl.multiple_of((first_row + c * zero_rows) * SUBLANES, SUBLANES)
            cp = pltpu.make_async_copy(zeros_ref, xs_hbm.at[pl.ds(first, zero_rows * SUBLANES)], sem)
            cp.start()
            cp.wait()

    @pl.when(i == 0)
    def _():
        zeros_ref[...] = jnp.zeros_like(zeros_ref)
        for e in range(N_EXPERTS):
            start, end = seg_ref[e], seg_ref[N_EXPERTS + e]
            pl.when(end > start)(functools.partial(clear_tile, end - tm))
            spare = seg_ref[2 * N_EXPERTS] + e
            pl.when(spare < n_tiles)(functools.partial(clear_tile, spare * tm))

    base = i * tc * TOP_K

    def issue(r, carry):
        for k in range(TOP_K):
            p = pos_ref[base + TOP_K * r + k]
            pltpu.make_async_copy(_tile(h_ref, r), _tile(xs_hbm, p), sem).start(priority=k)
        return carry
    lax.fori_loop(0, tc, issue, 0, unroll=8)
    _wait_tiles(TOP_K * tc, h_ref, xs_hbm, sem)


def _dispatch(h2t, pos, seg, n_rows, tm):
    m = h2t.shape[0] // SUBLANES
    tc = min(1024, m)
    return pl.pallas_call(
        functools.partial(_dispatch_kernel, tc=tc, tm=tm),
        grid_spec=pltpu.PrefetchScalarGridSpec(
            num_scalar_prefetch=2,
            grid=(m // tc,),
            in_specs=[pl.BlockSpec((tc * SUBLANES, LANES), lambda i, pos, seg: (i, 0))],
            out_specs=pl.BlockSpec(memory_space=pl.ANY),
            scratch_shapes=[pltpu.VMEM((min(tm, 256) * SUBLANES, LANES), F32), pltpu.SemaphoreType.DMA],
        ),
        out_shape=jax.ShapeDtypeStruct((n_rows * SUBLANES, LANES), F32),
        compiler_params=_cparams(1),
        name="moe_dispatch",
    )(pos, seg, h2t)


def _experts_kernel(te_ref, nused_ref, xs_ref, wgu_ref, wd_ref, ys_ref, *, tm, n_chunks):
    i = pl.program_id(0)
    d_ff = wd_ref.shape[1]
    tf = d_ff // n_chunks

    @pl.when(i < nused_ref[0])
    def _():
        x = jnp.concatenate([xs_ref[pl.ds(s, tm, stride=SUBLANES), :].astype(BF16)
                             for s in range(SUBLANES)], axis=1)
        acc = None
        for c in range(n_chunks):
            gate = _dot(x, wgu_ref[0, :, c * tf:(c + 1) * tf])
            up = _dot(x, wgu_ref[0, :, d_ff + c * tf:d_ff + (c + 1) * tf])
            part = _dot((_silu(gate) * up).astype(BF16), wd_ref[0, c * tf:(c + 1) * tf, :])
            acc = part if acc is None else acc + part
        for s in range(SUBLANES):
            ys_ref[pl.ds(s, tm, stride=SUBLANES), :] = acc[:, s * LANES:(s + 1) * LANES]

    @pl.when(i >= nused_ref[0])
    def _():
        ys_ref[...] = jnp.zeros_like(ys_ref)


def _experts(xs, tile_expert, n_used, w_gu, w_down, tm):
    n_rows = xs.shape[0] // SUBLANES
    resident = pl.Buffered(1)
    return pl.pallas_call(
        functools.partial(_experts_kernel, tm=tm, n_chunks=2),
        grid_spec=pltpu.PrefetchScalarGridSpec(
            num_scalar_prefetch=2,
            grid=(n_rows // tm,),
            in_specs=[
                pl.BlockSpec((tm * SUBLANES, LANES), lambda i, te, nu: (i, 0)),
                pl.BlockSpec((1,) + w_gu.shape[1:], lambda i, te, nu: (te[i], 0, 0), pipeline_mode=resident),
                pl.BlockSpec((1,) + w_down.shape[1:], lambda i, te, nu: (te[i], 0, 0)),
            ],
            out_specs=pl.BlockSpec((tm * SUBLANES, LANES), lambda i, te, nu: (i, 0)),
        ),
        out_shape=jax.ShapeDtypeStruct((n_rows * SUBLANES, LANES), F32),
        compiler_params=_cparams(1),
        name="moe_experts",
    )(tile_expert, n_used, xs, w_gu, w_down)


def _combine_kernel(pos_ref, ys_hbm, route_ref, x_ref, g_ref, mod_ref, o_ref, buf, sem, *, tc):
    i = pl.program_id(0)
    n = pl.num_programs(0)

    def gather(block, slot):
        base = block * tc * TOP_K

        def issue(r, carry):
            for k in range(TOP_K):
                p = pos_ref[base + TOP_K * r + k]
                pltpu.make_async_copy(_tile(ys_hbm, p), _tile(buf.at[slot, k], r), sem.at[slot]).start(priority=k)
            return carry
        lax.fori_loop(0, tc, issue, 0, unroll=8)

    @pl.when(i == 0)
    def _():
        gather(0, 0)

    @pl.when(i + 1 < n)
    def _():
        gather(i + 1, (i + 1) % 2)

    slot = i % 2
    _wait_tiles(TOP_K * tc, ys_hbm, buf.at[slot, 0], sem.at[slot])
    rows = lambda k: jnp.concatenate([buf[slot, k, pl.ds(s, tc, stride=SUBLANES), :] for s in range(SUBLANES)], axis=1)
    route = route_ref[...]
    f = route[:, 2:3] * rows(0) + route[:, 3:4] * rows(1)
    o_ref[...] = x_ref[...] + mod_ref[0, 0, 5:6, :] * _rmsnorm(f, g_ref[0, 3:4, :])


def _combine(ys, pos, route, x, norm_g, mods, layer, mod_row_of_block, tc):
    m, d = x.shape
    return pl.pallas_call(
        functools.partial(_combine_kernel, tc=tc),
        grid_spec=pltpu.PrefetchScalarGridSpec(
            num_scalar_prefetch=1,
            grid=(m // tc,),
            in_specs=[
                pl.BlockSpec(memory_space=pl.ANY),
                pl.BlockSpec((tc, LANES), lambda i, pos: (i, 0)),
                pl.BlockSpec((tc, d), lambda i, pos: (i, 0)),
                _norm_spec(norm_g, layer, 1),
                _mod_spec(mods, layer, lambda i, pos: mod_row_of_block(i)),
            ],
            out_specs=pl.BlockSpec((tc, d), lambda i, pos: (i, 0)),
            scratch_shapes=[pltpu.VMEM((2, TOP_K, tc * SUBLANES, LANES), F32),
                            pltpu.SemaphoreType.DMA((2,))],
        ),
        out_shape=jax.ShapeDtypeStruct((m, d), F32),
        compiler_params=_cparams(1),
        name="moe_combine",
    )(pos, ys, route, x, norm_g, mods)


def _moe(h2t, route, x, norm_g, mods, layer, mod_row, w_gu, w_down):
    b, l, d = x.shape
    m = b * l
    tm = min(512, l)
    tc = min(512, l)
    route2 = route.reshape(m, LANES)
    pos, tile_expert, n_used, seg = _route_plan(route2, tm)
    n_rows = TOP_K * m + N_EXPERTS * tm
    xs = _dispatch(h2t, pos, seg, n_rows, tm)
    ys = _experts(xs, tile_expert, n_used, w_gu, w_down, tm)
    blocks_per_seq = l // tc
    out = _combine(ys, pos, route2, x.reshape(m, d), norm_g, mods, layer,
                   lambda i: mod_row(i // blocks_per_seq), tc)
    return out.reshape(b, l, d)


def kernel(x, c, ctx, c_ctx, ada_w, ada_b, norm_g, attn_w_qkv, attn_w_o, attn_sink, pool_w, pool_b,
           pool_scale, ffn_w_gu, ffn_w_down, router_w, router_b, moe_w_gu, moe_w_down):
    batch, seq, d = x.shape
    depth = ada_w.shape[0]
    ctx_row = batch
    assert batch < MOD_ROWS and seq % ATTN_BLOCK == 0 and d == N_HEADS * HEAD_DIM

    cond = jnp.zeros((MOD_ROWS, d), F32).at[:batch].set(c).at[ctx_row].set(c_ctx)
    mods = _ada_mods(cond, ada_w, ada_b).reshape(depth, MOD_ROWS, 6, d)
    latent_row = lambda bi: bi
    context_row = lambda bi: ctx_row
    rope = _rope_tables(seq)

    w_qkv = attn_w_qkv.astype(BF16)
    w_o = attn_w_o.astype(BF16)
    w_gu = ffn_w_gu.astype(BF16)
    w_down = ffn_w_down.astype(BF16)
    p_w = pool_w.astype(BF16)
    p_b = pool_b.reshape(-1, 1, d)
    p_s = pool_scale.reshape(-1, 1, d)
    n_moe, n_exp, _, gu_cols = moe_w_gu.shape
    d_ff_e = moe_w_down.shape[2]
    gu_src = moe_w_gu.reshape(n_moe, n_exp * d, gu_cols)
    down_src = moe_w_down.reshape(n_moe, n_exp * d_ff_e, d)
    experts_bf16 = {}
    r_w = jnp.zeros((router_w.shape[0], d, LANES), BF16).at[:, :, :n_exp].set(router_w.astype(BF16))
    r_b = jnp.full((router_b.shape[0], 1, LANES), NEG_INF, F32).at[:, 0, :n_exp].set(router_b)

    is_attn = [i % 2 == 0 for i in range(depth)]
    for i in range(depth):
        slot = i // 2
        ctx_later = any(is_attn[i + 1:])
        if is_attn[i]:
            q, kv = _qkv_proj(x, norm_g, mods, i, latent_row, w_qkv, slot, rope)
            qc, kvc = _qkv_proj(ctx, norm_g, mods, i, context_row, w_qkv, slot, None)
            if i + 1 < depth and not is_attn[i + 1]:
                moe_slot = (i + 1) // 2
                attn, gu = _attention(q, kv, kvc, attn_sink, slot, True, cast=(gu_src, moe_slot))
                x, down = _oproj_ffn(attn, x, norm_g, mods, i, latent_row, w_o, w_gu, w_down, slot,
                                     cast=(down_src, moe_slot))
                experts_bf16[moe_slot] = (gu.reshape(n_exp, d, gu_cols), down.reshape(n_exp, d_ff_e, d))
            else:
                attn = _attention(q, kv, kvc, attn_sink, slot, True)
                x = _oproj_ffn(attn, x, norm_g, mods, i, latent_row, w_o, w_gu, w_down, slot)
            if ctx_later:
                attn_c = _attention(qc, None, kvc, attn_sink, slot, False)
                ctx = _oproj_ffn(attn_c, ctx, norm_g, mods, i, context_row, w_o, w_gu, w_down, slot)
        else:
            if slot not in experts_bf16:
                experts_bf16[slot] = (moe_w_gu[slot].astype(BF16), moe_w_down[slot].astype(BF16))
            e_gu, e_down = experts_bf16[slot]
            x1, h2t, route = _pool_mixer(x, norm_g, mods, i, latent_row, p_w, p_b, p_s, r_w, r_b, slot)
            x = _moe(h2t, route, x1, norm_g, mods, i, latent_row, e_gu, e_down)
            if ctx_later:
                c1, h2c, route_c = _pool_mixer(ctx, norm_g, mods, i, context_row, p_w, p_b, p_s, r_w, r_b, slot)
                ctx = _moe(h2c, route_c, c1, norm_g, mods, i, context_row, e_gu, e_down)
    return x
```

```python
import functools

import jax
import jax.numpy as jnp
from jax import lax
from jax.experimental import pallas as pl
from jax.experimental.pallas import tpu as pltpu

F32 = jnp.float32
BF16 = jnp.bfloat16

N_HEADS = 16
N_KV_HEADS = 4
HEAD_DIM = 64
GROUP = N_HEADS // N_KV_HEADS
ATTN_BLOCK = 128
GRID_W = 64
ROPE_BASE = 10000.0
ROPE_HALF = HEAD_DIM // 4
POOL_WINDOWS = (2, 4, 8, 16)
POOL_HALO = 8
N_EXPERTS = 8
TOP_K = 2
NORM_EPS = 1e-6
NEG_INF = -1e30
LOG2_E = 1.4426950408889634
Q_SCALE = HEAD_DIM ** -0.5 * LOG2_E
LANES = 128
SUBLANES = 8
MOD_ROWS = 8
VMEM_LIMIT = 56 * 1024 * 1024


def _cparams(n_axes):
    return pltpu.CompilerParams(dimension_semantics=("arbitrary",) * n_axes,
                                vmem_limit_bytes=VMEM_LIMIT)


def _rmsnorm(x, g):
    return x * lax.rsqrt(jnp.mean(x * x, axis=-1, keepdims=True) + NORM_EPS) * g


def _norm_mod(x, g, shift, scale):
    return _rmsnorm(x, g) * (1.0 + scale) + shift


def _silu(x):
    return x / (1.0 + jnp.exp(-x))


def _dot(a, b):
    return jnp.dot(a, b, preferred_element_type=F32)


def _norm_spec(norm_g, layer, n_grid):
    return pl.BlockSpec((1,) + norm_g.shape[1:], lambda *_: (layer, 0, 0))


def _mod_spec(mods, layer, row_of):
    return pl.BlockSpec((1, 1) + mods.shape[2:], lambda *idx: (layer, row_of(*idx), 0, 0))


def _slot_spec(w, slot, **kw):
    zeros = (0,) * (w.ndim - 1)
    return pl.BlockSpec((1,) + w.shape[1:], lambda *_: (slot,) + zeros, **kw)


def _ada_kernel(cond_ref, w_ref, b_ref, o_ref):
    s = _silu(cond_ref[...]).astype(BF16)
    o_ref[0] = _dot(s, w_ref[0].astype(BF16)) + b_ref[0]


def _ada_mods(cond, ada_w, ada_b):
    depth, d, n = ada_w.shape
    tn = 1536
    return pl.pallas_call(
        _ada_kernel,
        grid=(depth, n // tn),
        in_specs=[
            pl.BlockSpec((MOD_ROWS, d), lambda l, j: (0, 0)),
            pl.BlockSpec((1, d, tn), lambda l, j: (l, 0, j)),
            pl.BlockSpec((1, 1, tn), lambda l, j: (l, 0, j)),
        ],
        out_specs=pl.BlockSpec((1, MOD_ROWS, tn), lambda l, j: (l, 0, j)),
        out_shape=jax.ShapeDtypeStruct((depth, MOD_ROWS, n), F32),
        compiler_params=_cparams(2),
        name="ada_mods",
    )(cond, ada_w, ada_b.reshape(depth, 1, n))


def _rope(chunk, cos, sin, take_upper):
    rot = jnp.where(take_upper, pltpu.roll(chunk, LANES - ROPE_HALF, axis=1),
                    pltpu.roll(chunk, ROPE_HALF, axis=1))
    return chunk * cos + rot * sin


def _qkv_kernel(*refs, rope):
    if rope:
        x_ref, g_ref, mod_ref, w_ref, cos_ref, sin_ref, q_ref, kv_ref = refs
    else:
        x_ref, g_ref, mod_ref, w_ref, q_ref, kv_ref = refs
    h = _norm_mod(x_ref[0], g_ref[0, 0:1, :], mod_ref[0, 0, 0:1, :], mod_ref[0, 0, 1:2, :])
    y = _dot(h.astype(BF16), w_ref[0])
    q_dim = N_HEADS * HEAD_DIM
    kv_dim = N_KV_HEADS * HEAD_DIM
    if rope:
        cos, sin = cos_ref[...], sin_ref[...]
        lane = lax.broadcasted_iota(jnp.int32, cos.shape, 1)
        take_upper = (lane % (2 * ROPE_HALF)) < ROPE_HALF
    for c in range(q_dim // LANES):
        chunk = y[:, c * LANES:(c + 1) * LANES]
        if rope:
            chunk = _rope(chunk, cos, sin, take_upper)
        q_ref[0, :, c * LANES:(c + 1) * LANES] = (chunk * Q_SCALE).astype(BF16)
    for c in range(kv_dim // LANES):
        k = y[:, q_dim + c * LANES:q_dim + (c + 1) * LANES]
        if rope:
            k = _rope(k, cos, sin, take_upper)
        v = y[:, q_dim + kv_dim + c * LANES:q_dim + kv_dim + (c + 1) * LANES]
        kv_ref[0, :, c * LANES:(c + 1) * LANES] = k.astype(BF16)
        kv_ref[0, :, kv_dim + c * LANES:kv_dim + (c + 1) * LANES] = pltpu.roll(k, HEAD_DIM, axis=1).astype(BF16)
        kv_ref[0, :, 2 * kv_dim + c * LANES:2 * kv_dim + (c + 1) * LANES] = v.astype(BF16)
        kv_ref[0, :, 3 * kv_dim + c * LANES:3 * kv_dim + (c + 1) * LANES] = pltpu.roll(v, HEAD_DIM, axis=1).astype(BF16)


def _qkv_proj(x, norm_g, mods, layer, mod_row, w_qkv, slot, rope_tables):
    b, l, d = x.shape
    tm = min(512, l)
    rope = rope_tables is not None
    in_specs = [
        pl.BlockSpec((1, tm, d), lambda bi, i: (bi, i, 0)),
        _norm_spec(norm_g, layer, 2),
        _mod_spec(mods, layer, lambda bi, i: mod_row(bi)),
        _slot_spec(w_qkv, slot),
    ]
    args = [x, norm_g, mods, w_qkv]
    if rope:
        in_specs += [pl.BlockSpec((tm, LANES), lambda bi, i: (i, 0))] * 2
        args += list(rope_tables)
    q_dim = N_HEADS * HEAD_DIM
    return pl.pallas_call(
        functools.partial(_qkv_kernel, rope=rope),
        grid=(b, l // tm),
        in_specs=in_specs,
        out_specs=[pl.BlockSpec((1, tm, q_dim), lambda bi, i: (bi, i, 0)),
                   pl.BlockSpec((1, tm, q_dim), lambda bi, i: (bi, i, 0))],
        out_shape=[jax.ShapeDtypeStruct((b, l, q_dim), BF16),
                   jax.ShapeDtypeStruct((b, l, q_dim), BF16)],
        compiler_params=_cparams(2),
        name="qkv_rope" if rope else "qkv",
    )(*args)


def _rope_tables(n_tokens):
    t = jnp.arange(n_tokens)
    row = (t // GRID_W).astype(F32)
    col = (t % GRID_W).astype(F32)
    inv_freq = ROPE_BASE ** (-jnp.arange(ROPE_HALF, dtype=F32) / ROPE_HALF)
    ang_r = row[:, None] * inv_freq
    ang_c = col[:, None] * inv_freq
    cos_h = jnp.concatenate([jnp.cos(ang_r), jnp.cos(ang_r), jnp.cos(ang_c), jnp.cos(ang_c)], axis=-1)
    sin_h = jnp.concatenate([-jnp.sin(ang_r), jnp.sin(ang_r), -jnp.sin(ang_c), jnp.sin(ang_c)], axis=-1)
    reps = LANES // HEAD_DIM
    return jnp.tile(cos_h, (1, reps)), jnp.tile(sin_h, (1, reps))


def _pad_heads(dst_ref, rows, src):
    kv_dim = N_KV_HEADS * HEAD_DIM
    n = src.shape[0]
    lane = lax.broadcasted_iota(jnp.int32, (n, LANES), 1).astype(F32).astype(BF16)
    low = lane < HEAD_DIM
    zero = jnp.zeros((n, LANES), BF16)
    for t in range(2):
        for h in range(N_KV_HEADS):
            c = h // 2
            plain = src[:, 2 * t * kv_dim + c * LANES:2 * t * kv_dim + (c + 1) * LANES]
            swapped = src[:, (2 * t + 1) * kv_dim + c * LANES:(2 * t + 1) * kv_dim + (c + 1) * LANES]
            in_low, in_high = (plain, swapped) if h % 2 == 0 else (swapped, plain)
            base = t * 2 * N_KV_HEADS * LANES + h * 2 * LANES
            dst_ref[rows, base:base + LANES] = jnp.where(low, in_low, zero)
            dst_ref[rows, base + LANES:base + 2 * LANES] = jnp.where(low, zero, in_high)


def _lane_chunks(a):
    return [a[:, c * LANES:(c + 1) * LANES] for c in range(a.shape[1] // LANES)]


def _cast_spec(w, slot, n_steps, step_of):
    rows = w.shape[1] // n_steps
    assert rows * n_steps == w.shape[1] and rows % 16 == 0
    src = pl.BlockSpec((1, rows, w.shape[2]), lambda *idx: (slot, step_of(*idx), 0))
    dst = pl.BlockSpec((rows, w.shape[2]), lambda *idx: (step_of(*idx), 0))
    return src, dst, jax.ShapeDtypeStruct(w.shape[1:], BF16)


def _attn_kernel(*refs, tq, n_local_blocks, has_local, slot, side_cast):
    n_in = (6 if has_local else 3) + (1 if side_cast else 0)
    ins, rest = refs[:n_in], list(refs[n_in:])
    if has_local:
        sink_ref, q_ref, kvo_ref, kvp_ref, kvn_ref, kvc_ref = ins[:6]
    else:
        sink_ref, q_ref, kvc_ref = ins[:3]
    o_ref = rest.pop(0)
    if side_cast:
        rest.pop(0)[...] = ins[-1][0].astype(BF16)
    if has_local:
        loc_pad, ctx_pad = rest
    else:
        (ctx_pad,) = rest
    blk = ATTN_BLOCK
    i = pl.program_id(1)
    n_ctx = kvc_ref.shape[1]
    v_base = 2 * N_KV_HEADS * LANES
    _pad_heads(ctx_pad, slice(0, n_ctx), kvc_ref[0])
    if has_local:
        _pad_heads(loc_pad, slice(0, blk), kvp_ref[0])
        _pad_heads(loc_pad, slice(blk, blk + tq), kvo_ref[0])
        _pad_heads(loc_pad, slice(blk + tq, 2 * blk + tq), kvn_ref[0])

    lane_low = lax.broadcasted_iota(jnp.int32, (2 * blk, LANES), 1) < HEAD_DIM
    row_first = lax.broadcasted_iota(jnp.int32, (2 * blk, 1), 0) < blk

    def q_block(j, carry):
        r0 = pl.multiple_of(j * blk, blk)
        if has_local:
            n = i * (tq // blk) + j
            qi = lax.broadcasted_iota(jnp.int32, (2 * blk, 3 * blk), 0) & (blk - 1)
            ki = lax.broadcasted_iota(jnp.int32, (2 * blk, 3 * blk), 1)
            rel = ki - blk - qi
            k_lo = jnp.where(n == 0, blk, 0)
            k_hi = jnp.where(n == n_local_blocks - 1, 2 * blk, 3 * blk)
            valid = (rel >= -blk) & (rel <= blk) & (ki >= k_lo) & (ki < k_hi)
        for h in range(N_KV_HEADS):
            qp = q_ref[0, pl.ds(r0, blk), h * 2 * LANES:(h + 1) * 2 * LANES]
            lhs = jnp.concatenate([qp[:, :LANES], qp[:, LANES:]], axis=0)
            kc = h * 2 * LANES
            vc = v_base + h * 2 * LANES
            nt = (((1,), (1,)), ((), ()))
            seg_scores = [[], []]
            k_ctx = jnp.concatenate([ctx_pad[:, kc:kc + LANES], ctx_pad[:, kc + LANES:kc + 2 * LANES]], axis=0)
            s_ctx = lax.dot_general(lhs, k_ctx, nt, preferred_element_type=F32)
            seg_scores[0].append(s_ctx[:, :n_ctx])
            seg_scores[1].append(s_ctx[:, n_ctx:])
            if has_local:
                k_loc = jnp.concatenate([loc_pad[pl.ds(r0, 3 * blk), kc:kc + LANES],
                                         loc_pad[pl.ds(r0, 3 * blk), kc + LANES:kc + 2 * LANES]], axis=0)
                s_loc = lax.dot_general(lhs, k_loc, nt, preferred_element_type=F32)
                seg_scores[0].append(jnp.where(valid, s_loc[:, :3 * blk], NEG_INF))
                seg_scores[1].append(jnp.where(valid, s_loc[:, 3 * blk:], NEG_INF))
            probs, inv = [], []
            for seg in range(2):
                sink = jnp.where(row_first, sink_ref[slot, h * GROUP + seg],
                                 sink_ref[slot, h * GROUP + 2 + seg]) * LOG2_E
                col_max = functools.reduce(jnp.maximum, [c for s in seg_scores[seg] for c in _lane_chunks(s)])
                m = jnp.maximum(sink, jnp.max(col_max, axis=-1, keepdims=True))
                ps = [jnp.exp2(s - m) for s in seg_scores[seg]]
                col_sum = functools.reduce(jnp.add, [c for p in ps for c in _lane_chunks(p)])
                denom = jnp.exp2(sink - m) + jnp.sum(col_sum, axis=-1, keepdims=True)
                probs.append(ps)
                inv.append(1.0 / denom)
            p_ctx = jnp.concatenate([probs[0][0], probs[1][0]], axis=1).astype(BF16)
            v_ctx = jnp.concatenate([ctx_pad[:, vc:vc + LANES], ctx_pad[:, vc + LANES:vc + 2 * LANES]], axis=0)
            o = _dot(p_ctx, v_ctx)
            if has_local:
                p_loc = jnp.concatenate([probs[0][1], probs[1][1]], axis=1).astype(BF16)
                v_loc = jnp.concatenate([loc_pad[pl.ds(r0, 3 * blk), vc:vc + LANES],
                                         loc_pad[pl.ds(r0, 3 * blk), vc + LANES:vc + 2 * LANES]], axis=0)
                o = o + _dot(p_loc, v_loc)
            o = o * jnp.where(lane_low, inv[0], inv[1])
            o_ref[0, pl.ds(r0, blk), h * 2 * LANES:h * 2 * LANES + LANES] = o[:blk].astype(BF16)
            o_ref[0, pl.ds(r0, blk), h * 2 * LANES + LANES:(h + 1) * 2 * LANES] = o[blk:].astype(BF16)
        return carry

    lax.fori_loop(0, tq // blk, q_block, 0, unroll=2)


def _attention(q, kv, kv_ctx, sink, slot, has_local, cast=None):
    b, l, q_dim = q.shape
    n_ctx = kv_ctx.shape[1]
    blk = ATTN_BLOCK
    tq = min(1024, l)
    bpt = tq // blk
    n_blocks = l // blk
    pad_lanes = 4 * N_KV_HEADS * LANES
    in_specs = [pl.BlockSpec(memory_space=pltpu.SMEM),
                pl.BlockSpec((1, tq, q_dim), lambda bi, i: (bi, i, 0))]
    args = [sink, q]
    scratch = []
    if has_local:
        in_specs += [
            pl.BlockSpec((1, tq, q_dim), lambda bi, i: (bi, i, 0)),
            pl.BlockSpec((1, blk, q_dim), lambda bi, i: (bi, jnp.maximum(i * bpt - 1, 0), 0)),
            pl.BlockSpec((1, blk, q_dim), lambda bi, i: (bi, jnp.minimum((i + 1) * bpt, n_blocks - 1), 0)),
        ]
        args += [kv, kv, kv]
        scratch.append(pltpu.VMEM((tq + 2 * blk, pad_lanes), BF16))
    in_specs.append(pl.BlockSpec((1, n_ctx, q_dim), lambda bi, i: (bi, 0, 0)))
    args.append(kv_ctx)
    scratch.append(pltpu.VMEM((n_ctx, pad_lanes), BF16))
    out_specs = [pl.BlockSpec((1, tq, q_dim), lambda bi, i: (bi, i, 0))]
    out_shape = [jax.ShapeDtypeStruct((b, l, q_dim), BF16)]
    if cast is not None:
        n_i = l // tq
        src, dst, shape = _cast_spec(cast[0], cast[1], b * n_i, lambda bi, i: bi * n_i + i)
        in_specs.append(src)
        args.append(cast[0])
        out_specs.append(dst)
        out_shape.append(shape)
    out = pl.pallas_call(
        functools.partial(_attn_kernel, tq=tq, n_local_blocks=n_blocks, has_local=has_local, slot=slot,
                          side_cast=cast is not None),
        grid=(b, l // tq),
        in_specs=in_specs,
        out_specs=out_specs,
        out_shape=out_shape,
        scratch_shapes=scratch,
        compiler_params=_cparams(2),
        name="window_attention" if has_local else "context_attention",
    )(*args)
    return out if cast is not None else out[0]


def _oproj_ffn_kernel(a_ref, x_ref, g_ref, mod_ref, wo_ref, wgu_ref, wd_ref, *rest):
    if len(rest) == 3:
        cast_src, o_ref, cast_dst = rest
        cast_dst[...] = cast_src[0].astype(BF16)
    else:
        (o_ref,) = rest
    d_ff = wd_ref.shape[1]
    mod = lambda r: mod_ref[0, 0, r:r + 1, :]
    y = _dot(a_ref[0], wo_ref[0])
    x = x_ref[0] + mod(2) * _rmsnorm(y, g_ref[0, 1:2, :])
    h = _norm_mod(x, g_ref[0, 2:3, :], mod(3), mod(4)).astype(BF16)
    gate = _dot(h, wgu_ref[0, :, :d_ff])
    up = _dot(h, wgu_ref[0, :, d_ff:])
    f = _dot((_silu(gate) * up).astype(BF16), wd_ref[0])
    o_ref[0] = x + mod(5) * _rmsnorm(f, g_ref[0, 3:4, :])


def _oproj_ffn(attn, x, norm_g, mods, layer, mod_row, w_o, w_gu, w_down, slot, cast=None):
    b, l, d = x.shape
    tm = min(512, l)
    resident = pl.Buffered(1)
    in_specs = [
        pl.BlockSpec((1, tm, attn.shape[2]), lambda bi, i: (bi, i, 0)),
        pl.BlockSpec((1, tm, d), lambda bi, i: (bi, i, 0)),
        _norm_spec(norm_g, layer, 2),
        _mod_spec(mods, layer, lambda bi, i: mod_row(bi)),
        _slot_spec(w_o, slot, pipeline_mode=resident),
        _slot_spec(w_gu, slot, pipeline_mode=resident),
        _slot_spec(w_down, slot, pipeline_mode=resident),
    ]
    args = [attn, x, norm_g, mods, w_o, w_gu, w_down]
    out_specs = [pl.BlockSpec((1, tm, d), lambda bi, i: (bi, i, 0))]
    out_shape = [jax.ShapeDtypeStruct((b, l, d), F32)]
    if cast is not None:
        n_i = l // tm
        src, dst, shape = _cast_spec(cast[0], cast[1], b * n_i, lambda bi, i: bi * n_i + i)
        in_specs.append(src)
        args.append(cast[0])
        out_specs.append(dst)
        out_shape.append(shape)
    out = pl.pallas_call(
        _oproj_ffn_kernel,
        grid=(b, l // tm),
        in_specs=in_specs,
        out_specs=out_specs,
        out_shape=out_shape,
        compiler_params=_cparams(2),
        name="oproj_ffn",
    )(*args)
    return out if cast is not None else out[0]


def _pool_kernel(x_ref, xp_ref, xn_ref, g_ref, mod_ref, pw_ref, pb_ref, ps_ref, rw_ref, rb_ref,
                 xo_ref, h2_ref, route_ref, hbuf, ybuf, *, tm, seq_len):
    i = pl.program_id(1)
    n_blk = pl.num_programs(1)
    sub = 128
    halo = POOL_HALO
    gd = pw_ref.shape[2]
    mod = lambda r: mod_ref[0, 0, r:r + 1, :]
    g_pre, shift, scale = g_ref[0, 0:1, :], mod(0), mod(1)
    x = x_ref[0]
    hbuf[0:halo, :] = _norm_mod(xp_ref[0], g_pre, shift, scale) * jnp.where(i > 0, 1.0, 0.0)
    hbuf[halo:halo + tm, :] = _norm_mod(x, g_pre, shift, scale)
    hbuf[halo + tm:, :] = _norm_mod(xn_ref[0], g_pre, shift, scale) * jnp.where(i < n_blk - 1, 1.0, 0.0)

    r = lax.broadcasted_iota(jnp.int32, (sub, 2 * sub), 0)
    c = lax.broadcasted_iota(jnp.int32, (sub, 2 * sub), 1)
    t_row = lax.broadcasted_iota(jnp.int32, (sub, 1), 0)
    fill = jnp.zeros((sub - 2 * halo, gd), BF16)
    for gi, w in enumerate(POOL_WINDOWS):
        band = jnp.where((c >= r + halo - w // 2) & (c < r + halo - w // 2 + w), 1.0, 0.0).astype(BF16)
        for s in range(tm // sub):
            hs = hbuf[s * sub:s * sub + sub + 2 * halo, gi * gd:(gi + 1) * gd]
            hi = hs.astype(BF16)
            lo = (hs - hi.astype(F32)).astype(BF16)
            total = (_dot(band, jnp.concatenate([hi, fill], axis=0))
                     + _dot(band, jnp.concatenate([lo, fill], axis=0)))
            t = i * tm + s * sub + t_row
            count = jnp.clip(t - w // 2 + w, 0, seq_len) - jnp.clip(t - w // 2, 0, seq_len)
            centre = hbuf[halo + s * sub:halo + (s + 1) * sub, gi * gd:(gi + 1) * gd]
            diff = total * (1.0 / count.astype(F32)) - centre
            y = _dot(diff.astype(BF16), pw_ref[0, gi]) + pb_ref[0, :, gi * gd:(gi + 1) * gd]
            ybuf[s * sub:(s + 1) * sub, gi * gd:(gi + 1) * gd] = y * ps_ref[0, :, gi * gd:(gi + 1) * gd]

    xo = x + mod(2) * _rmsnorm(ybuf[...], g_ref[0, 1:2, :])
    xo_ref[0] = xo
    h2 = _norm_mod(xo, g_ref[0, 2:3, :], mod(3), mod(4))
    for s in range(SUBLANES):
        h2_ref[pl.ds(s, tm, stride=SUBLANES), :] = h2[:, s * LANES:(s + 1) * LANES]

    logits = _dot(h2.astype(BF16), rw_ref[0]) + rb_ref[0]
    lane = lax.broadcasted_iota(jnp.int32, logits.shape, 1)
    m1 = jnp.max(logits, axis=-1, keepdims=True)
    i1 = jnp.min(jnp.where(logits == m1, lane, LANES), axis=-1, keepdims=True)
    rest = jnp.where(lane == i1, NEG_INF * 2, logits)
    m2 = jnp.max(rest, axis=-1, keepdims=True)
    i2 = jnp.min(jnp.where(rest == m2, lane, LANES), axis=-1, keepdims=True)
    e = jnp.exp(m2 - m1)
    g1 = 1.0 / (1.0 + e)
    g2 = e / (1.0 + e)
    route_ref[0] = jnp.where(lane == 0, i1.astype(F32),
                             jnp.where(lane == 1, i2.astype(F32),
                                       jnp.where(lane == 2, g1, jnp.where(lane == 3, g2, 0.0))))


def _pool_mixer(x, norm_g, mods, layer, mod_row, pool_w, pool_b, pool_scale, router_w, router_b, slot):
    b, l, d = x.shape
    assert d == SUBLANES * LANES
    tm = min(512, l)
    hb = tm // POOL_HALO
    n_halo_blocks = l // POOL_HALO
    n_blk = l // tm
    row = lambda bi, i: (bi, i, 0)
    return pl.pallas_call(
        functools.partial(_pool_kernel, tm=tm, seq_len=l),
        grid=(b, n_blk),
        in_specs=[
            pl.BlockSpec((1, tm, d), row),
            pl.BlockSpec((1, POOL_HALO, d), lambda bi, i: (bi, jnp.maximum(i * hb - 1, 0), 0)),
            pl.BlockSpec((1, POOL_HALO, d), lambda bi, i: (bi, jnp.minimum((i + 1) * hb, n_halo_blocks - 1), 0)),
            _norm_spec(norm_g, layer, 2),
            _mod_spec(mods, layer, lambda bi, i: mod_row(bi)),
            _slot_spec(pool_w, slot),
            _slot_spec(pool_b, slot),
            _slot_spec(pool_scale, slot),
            _slot_spec(router_w, slot),
            _slot_spec(router_b, slot),
        ],
        out_specs=[pl.BlockSpec((1, tm, d), row),
                   pl.BlockSpec((tm * SUBLANES, LANES), lambda bi, i: (bi * n_blk + i, 0)),
                   pl.BlockSpec((1, tm, LANES), row)],
        out_shape=[jax.ShapeDtypeStruct((b, l, d), F32),
                   jax.ShapeDtypeStruct((b * l * SUBLANES, LANES), F32),
                   jax.ShapeDtypeStruct((b, l, LANES), F32)],
        scratch_shapes=[pltpu.VMEM((tm + 2 * POOL_HALO, d), F32), pltpu.VMEM((tm, d), F32)],
        compiler_params=_cparams(2),
        name="pool_mixer",
    )(x, x, x, norm_g, mods, pool_w, pool_b, pool_scale, router_w, router_b)


def _route_plan(route, tm):
    m = route.shape[0]
    n_tiles = (TOP_K * m) // tm + N_EXPERTS
    expert = route[:, :TOP_K].astype(jnp.int32).reshape(-1)
    onehot = (expert[:, None] == jnp.arange(N_EXPERTS, dtype=jnp.int32)[None, :]).astype(jnp.int32)
    csum = jnp.cumsum(onehot, axis=0)
    counts = csum[-1]
    rank = jnp.sum(csum * onehot, axis=1) - 1
    padded = ((counts + tm - 1) // tm) * tm
    ends = jnp.cumsum(padded)
    starts = ends - padded
    pos = jnp.sum(starts[None, :] * onehot, axis=1) + rank
    tile_start = jnp.arange(n_tiles, dtype=jnp.int32) * tm
    tile_expert = jnp.minimum(jnp.sum((tile_start[:, None] >= ends[None, :]).astype(jnp.int32), axis=1),
                              N_EXPERTS - 1)
    n_used = (ends[-1] // tm).reshape(1)
    i32 = lambda a: a.astype(jnp.int32)
    return i32(pos), i32(tile_expert), i32(n_used), i32(jnp.concatenate([starts, ends, n_used]))


def _tile(ref, row):
    return ref.at[pl.ds(pl.multiple_of(row * SUBLANES, SUBLANES), SUBLANES)]


def _wait_tiles(n, src_ref, dst_ref, sem):
    def wait(r, carry):
        pltpu.make_async_copy(_tile(src_ref, 0), _tile(dst_ref, 0), sem).wait()
        return carry
    lax.fori_loop(0, n, wait, 0, unroll=8)


def _dispatch_kernel(pos_ref, seg_ref, h_ref, xs_hbm, zeros_ref, sem, *, tc, tm):
    i = pl.program_id(0)
    zero_rows = zeros_ref.shape[0] // SUBLANES
    n_tiles = xs_hbm.shape[0] // (tm * SUBLANES)

    def clear_tile(first_row):
        for c in range(tm // zero_rows):
            first = pl.multiple_of((first_row + c * zero_rows) * SUBLANES, SUBLANES)
            cp = pltpu.make_async_copy(zeros_ref, xs_hbm.at[pl.ds(first, zero_rows * SUBLANES)], sem)
            cp.start()
            cp.wait()

    @pl.when(i == 0)
    def _():
        zeros_ref[...] = jnp.zeros_like(zeros_ref)
        for e in range(N_EXPERTS):
            start, end = seg_ref[e], seg_ref[N_EXPERTS + e]
            pl.when(end > start)(functools.partial(clear_tile, end - tm))
            spare = seg_ref[2 * N_EXPERTS] + e
            pl.when(spare < n_tiles)(functools.partial(clear_tile, spare * tm))

    base = i * tc * TOP_K

    def issue(r, carry):
        for k in range(TOP_K):
            p = pos_ref[base + TOP_K * r + k]
            pltpu.make_async_copy(_tile(h_ref, r), _tile(xs_hbm, p), sem).start(priority=k)
        return carry
    lax.fori_loop(0, tc, issue, 0, unroll=8)
    _wait_tiles(TOP_K * tc, h_ref, xs_hbm, sem)


def _dispatch(h2t, pos, seg, n_rows, tm):
    m = h2t.shape[0] // SUBLANES
    tc = min(2048, m)
    return pl.pallas_call(
        functools.partial(_dispatch_kernel, tc=tc, tm=tm),
        grid_spec=pltpu.PrefetchScalarGridSpec(
            num_scalar_prefetch=2,
            grid=(m // tc,),
            in_specs=[pl.BlockSpec((tc * SUBLANES, LANES), lambda i, pos, seg: (i, 0))],
            out_specs=pl.BlockSpec(memory_space=pl.ANY),
            scratch_shapes=[pltpu.VMEM((min(tm, 256) * SUBLANES, LANES), F32), pltpu.SemaphoreType.DMA],
        ),
        out_shape=jax.ShapeDtypeStruct((n_rows * SUBLANES, LANES), F32),
        compiler_params=_cparams(1),
        name="moe_dispatch",
    )(pos, seg, h2t)


def _experts_kernel(te_ref, nused_ref, xs_ref, wgu_ref, wd_ref, ys_ref, *, tm, n_chunks):
    i = pl.program_id(0)
    d_ff = wd_ref.shape[1]
    tf = d_ff // n_chunks

    @pl.when(i < nused_ref[0])
    def _():
        x = jnp.concatenate([xs_ref[pl.ds(s, tm, stride=SUBLANES), :].astype(BF16)
                             for s in range(SUBLANES)], axis=1)
        acc = None
        for c in range(n_chunks):
            gate = _dot(x, wgu_ref[0, :, c * tf:(c + 1) * tf])
            up = _dot(x, wgu_ref[0, :, d_ff + c * tf:d_ff + (c + 1) * tf])
            part = _dot((_silu(gate) * up).astype(BF16), wd_ref[0, c * tf:(c + 1) * tf, :])
            acc = part if acc is None else acc + part
        for s in range(SUBLANES):
            ys_ref[pl.ds(s, tm, stride=SUBLANES), :] = acc[:, s * LANES:(s + 1) * LANES]

    @pl.when(i >= nused_ref[0])
    def _():
        ys_ref[...] = jnp.zeros_like(ys_ref)


def _experts(xs, tile_expert, n_used, w_gu, w_down, tm):
    n_rows = xs.shape[0] // SUBLANES
    resident = pl.Buffered(1)
    return pl.pallas_call(
        functools.partial(_experts_kernel, tm=tm, n_chunks=2),
        grid_spec=pltpu.PrefetchScalarGridSpec(
            num_scalar_prefetch=2,
            grid=(n_rows // tm,),
            in_specs=[
                pl.BlockSpec((tm * SUBLANES, LANES), lambda i, te, nu: (i, 0)),
                pl.BlockSpec((1,) + w_gu.shape[1:], lambda i, te, nu: (te[i], 0, 0), pipeline_mode=resident),
                pl.BlockSpec((1,) + w_down.shape[1:], lambda i, te, nu: (te[i], 0, 0)),
            ],
            out_specs=pl.BlockSpec((tm * SUBLANES, LANES), lambda i, te, nu: (i, 0)),
        ),
        out_shape=jax.ShapeDtypeStruct((n_rows * SUBLANES, LANES), F32),
        compiler_params=_cparams(1),
        name="moe_experts",
    )(tile_expert, n_used, xs, w_gu, w_down)


def _combine_kernel(pos_ref, ys_hbm, route_ref, x_ref, g_ref, mod_ref, o_ref, buf, sem, *, tc):
    i = pl.program_id(0)
    n = pl.num_programs(0)

    def gather(block, slot):
        base = block * tc * TOP_K

        def issue(r, carry):
            for k in range(TOP_K):
                p = pos_ref[base + TOP_K * r + k]
                pltpu.make_async_copy(_tile(ys_hbm, p), _tile(buf.at[slot, k], r), sem.at[slot]).start(priority=k)
            return carry
        lax.fori_loop(0, tc, issue, 0, unroll=8)

    @pl.when(i == 0)
    def _():
        gather(0, 0)

    @pl.when(i + 1 < n)
    def _():
        gather(i + 1, (i + 1) % 2)

    slot = i % 2
    _wait_tiles(TOP_K * tc, ys_hbm, buf.at[slot, 0], sem.at[slot])
    rows = lambda k: jnp.concatenate([buf[slot, k, pl.ds(s, tc, stride=SUBLANES), :] for s in range(SUBLANES)], axis=1)
    route = route_ref[...]
    f = route[:, 2:3] * rows(0) + route[:, 3:4] * rows(1)
    o_ref[...] = x_ref[...] + mod_ref[0, 0, 5:6, :] * _rmsnorm(f, g_ref[0, 3:4, :])


def _combine(ys, pos, route, x, norm_g, mods, layer, mod_row_of_block, tc):
    m, d = x.shape
    return pl.pallas_call(
        functools.partial(_combine_kernel, tc=tc),
        grid_spec=pltpu.PrefetchScalarGridSpec(
            num_scalar_prefetch=1,
            grid=(m // tc,),
            in_specs=[
                pl.BlockSpec(memory_space=pl.ANY),
                pl.BlockSpec((tc, LANES), lambda i, pos: (i, 0)),
                pl.BlockSpec((tc, d), lambda i, pos: (i, 0)),
                _norm_spec(norm_g, layer, 1),
                _mod_spec(mods, layer, lambda i, pos: mod_row_of_block(i)),
            ],
            out_specs=pl.BlockSpec((tc, d), lambda i, pos: (i, 0)),
            scratch_shapes=[pltpu.VMEM((2, TOP_K, tc * SUBLANES, LANES), F32),
                            pltpu.SemaphoreType.DMA((2,))],
        ),
        out_shape=jax.ShapeDtypeStruct((m, d), F32),
        compiler_params=_cparams(1),
        name="moe_combine",
    )(pos, ys, route, x, norm_g, mods)


def _moe(h2t, route, x, norm_g, mods, layer, mod_row, w_gu, w_down):
    b, l, d = x.shape
    m = b * l
    tm = min(512, l)
    tc = min(1024, l)
    route2 = route.reshape(m, LANES)
    pos, tile_expert, n_used, seg = _route_plan(route2, tm)
    n_rows = TOP_K * m + N_EXPERTS * tm
    xs = _dispatch(h2t, pos, seg, n_rows, tm)
    ys = _experts(xs, tile_expert, n_used, w_gu, w_down, tm)
    blocks_per_seq = l // tc
    out = _combine(ys, pos, route2, x.reshape(m, d), norm_g, mods, layer,
                   lambda i: mod_row(i // blocks_per_seq), tc)
    return out.reshape(b, l, d)


def kernel(x, c, ctx, c_ctx, ada_w, ada_b, norm_g, attn_w_qkv, attn_w_o, attn_sink, pool_w, pool_b,
           pool_scale, ffn_w_gu, ffn_w_down, router_w, router_b, moe_w_gu, moe_w_down):
    batch, seq, d = x.shape
    depth = ada_w.shape[0]
    ctx_row = batch
    assert batch < MOD_ROWS and seq % ATTN_BLOCK == 0 and d == N_HEADS * HEAD_DIM

    cond = jnp.zeros((MOD_ROWS, d), F32).at[:batch].set(c).at[ctx_row].set(c_ctx)
    mods = _ada_mods(cond, ada_w, ada_b).reshape(depth, MOD_ROWS, 6, d)
    latent_row = lambda bi: bi
    context_row = lambda bi: ctx_row
    rope = _rope_tables(seq)

    w_qkv = attn_w_qkv.astype(BF16)
    w_o = attn_w_o.astype(BF16)
    w_gu = ffn_w_gu.astype(BF16)
    w_down = ffn_w_down.astype(BF16)
    p_w = pool_w.astype(BF16)
    p_b = pool_b.reshape(-1, 1, d)
    p_s = pool_scale.reshape(-1, 1, d)
    n_moe, n_exp, _, gu_cols = moe_w_gu.shape
    d_ff_e = moe_w_down.shape[2]
    gu_src = moe_w_gu.reshape(n_moe, n_exp * d, gu_cols)
    down_src = moe_w_down.reshape(n_moe, n_exp * d_ff_e, d)
    experts_bf16 = {}
    r_w = jnp.zeros((router_w.shape[0], d, LANES), BF16).at[:, :, :n_exp].set(router_w.astype(BF16))
    r_b = jnp.full((router_b.shape[0], 1, LANES), NEG_INF, F32).at[:, 0, :n_exp].set(router_b)

    is_attn = [i % 2 == 0 for i in range(depth)]
    for i in range(depth):
        slot = i // 2
        ctx_later = any(is_attn[i + 1:])
        if is_attn[i]:
            q, kv = _qkv_proj(x, norm_g, mods, i, latent_row, w_qkv, slot, rope)
            qc, kvc = _qkv_proj(ctx, norm_g, mods, i, context_row, w_qkv, slot, None)
            if i + 1 < depth and not is_attn[i + 1]:
                moe_slot = (i + 1) // 2
                attn, gu = _attention(q, kv, kvc, attn_sink, slot, True, cast=(gu_src, moe_slot))
                x, down = _oproj_ffn(attn, x, norm_g, mods, i, latent_row, w_o, w_gu, w_down, slot,
                                     cast=(down_src, moe_slot))
                experts_bf16[moe_slot] = (gu.reshape(n_exp, d, gu_cols), down.reshape(n_exp, d_ff_e, d))
            else:
                attn = _attention(q, kv, kvc, attn_sink, slot, True)
                x = _oproj_ffn(attn, x, norm_g, mods, i, latent_row, w_o, w_gu, w_down, slot)
            if ctx_later:
                attn_c = _attention(qc, None, kvc, attn_sink, slot, False)
                ctx = _oproj_ffn(attn_c, ctx, norm_g, mods, i, context_row, w_o, w_gu, w_down, slot)
        else:
            if slot not in experts_bf16:
                experts_bf16[slot] = (moe_w_gu[slot].astype(BF16), moe_w_down[slot].astype(BF16))
            e_gu, e_down = experts_bf16[slot]
            x1, h2t, route = _pool_mixer(x, norm_g, mods, i, latent_row, p_w, p_b, p_s, r_w, r_b, slot)
            x = _moe(h2t, route, x1, norm_g, mods, i, latent_row, e_gu, e_down)
            if ctx_later:
                c1, h2c, route_c = _pool_mixer(ctx, norm_g, mods, i, context_row, p_w, p_b, p_s, r_w, r_b, slot)
                ctx = _moe(h2c, route_c, c1, norm_g, mods, i, context_row, e_gu, e_down)
    return x
```

```python
import functools

import jax
import jax.numpy as jnp
from jax import lax
from jax.experimental import pallas as pl
from jax.experimental.pallas import tpu as pltpu

F32 = jnp.float32
BF16 = jnp.bfloat16

N_HEADS = 16
N_KV_HEADS = 4
HEAD_DIM = 64
GROUP = N_HEADS // N_KV_HEADS
ATTN_BLOCK = 128
GRID_W = 64
ROPE_BASE = 10000.0
ROPE_HALF = HEAD_DIM // 4
POOL_WINDOWS = (2, 4, 8, 16)
POOL_HALO = 8
N_EXPERTS = 8
TOP_K = 2
NORM_EPS = 1e-6
NEG_INF = -1e30
LOG2_E = 1.4426950408889634
Q_SCALE = HEAD_DIM ** -0.5 * LOG2_E
LANES = 128
SUBLANES = 8
MOD_ROWS = 8
VMEM_LIMIT = 56 * 1024 * 1024


def _cparams(n_axes):
    return pltpu.CompilerParams(dimension_semantics=("arbitrary",) * n_axes,
                                vmem_limit_bytes=VMEM_LIMIT)


def _rmsnorm(x, g):
    return x * lax.rsqrt(jnp.mean(x * x, axis=-1, keepdims=True) + NORM_EPS) * g


def _norm_mod(x, g, shift, scale):
    return _rmsnorm(x, g) * (1.0 + scale) + shift


def _silu(x):
    return x / (1.0 + jnp.exp(-x))


def _dot(a, b):
    return jnp.dot(a, b, preferred_element_type=F32)


def _norm_spec(norm_g, layer, n_grid):
    return pl.BlockSpec((1,) + norm_g.shape[1:], lambda *_: (layer, 0, 0))


def _mod_spec(mods, layer, row_of):
    return pl.BlockSpec((1, 1) + mods.shape[2:], lambda *idx: (layer, row_of(*idx), 0, 0))


def _slot_spec(w, slot, **kw):
    zeros = (0,) * (w.ndim - 1)
    return pl.BlockSpec((1,) + w.shape[1:], lambda *_: (slot,) + zeros, **kw)


def _ada_kernel(cond_ref, w_ref, b_ref, o_ref):
    s = _silu(cond_ref[...]).astype(BF16)
    o_ref[0] = _dot(s, w_ref[0].astype(BF16)) + b_ref[0]


def _ada_mods(cond, ada_w, ada_b):
    depth, d, n = ada_w.shape
    tn = 1536
    return pl.pallas_call(
        _ada_kernel,
        grid=(depth, n // tn),
        in_specs=[
            pl.BlockSpec((MOD_ROWS, d), lambda l, j: (0, 0)),
            pl.BlockSpec((1, d, tn), lambda l, j: (l, 0, j)),
            pl.BlockSpec((1, 1, tn), lambda l, j: (l, 0, j)),
        ],
        out_specs=pl.BlockSpec((1, MOD_ROWS, tn), lambda l, j: (l, 0, j)),
        out_shape=jax.ShapeDtypeStruct((depth, MOD_ROWS, n), F32),
        compiler_params=_cparams(2),
        name="ada_mods",
    )(cond, ada_w, ada_b.reshape(depth, 1, n))


def _rope(chunk, cos, sin, take_upper):
    rot = jnp.where(take_upper, pltpu.roll(chunk, LANES - ROPE_HALF, axis=1),
                    pltpu.roll(chunk, ROPE_HALF, axis=1))
    return chunk * cos + rot * sin


def _qkv_kernel(*refs, rope):
    if rope:
        x_ref, g_ref, mod_ref, w_ref, cos_ref, sin_ref, q_ref, kv_ref = refs
    else:
        x_ref, g_ref, mod_ref, w_ref, q_ref, kv_ref = refs
    h = _norm_mod(x_ref[0], g_ref[0, 0:1, :], mod_ref[0, 0, 0:1, :], mod_ref[0, 0, 1:2, :])
    y = _dot(h.astype(BF16), w_ref[0])
    q_dim = N_HEADS * HEAD_DIM
    kv_dim = N_KV_HEADS * HEAD_DIM
    if rope:
        cos, sin = cos_ref[...], sin_ref[...]
        lane = lax.broadcasted_iota(jnp.int32, cos.shape, 1)
        take_upper = (lane % (2 * ROPE_HALF)) < ROPE_HALF
    for c in range(q_dim // LANES):
        chunk = y[:, c * LANES:(c + 1) * LANES]
        if rope:
            chunk = _rope(chunk, cos, sin, take_upper)
        q_ref[0, :, c * LANES:(c + 1) * LANES] = (chunk * Q_SCALE).astype(BF16)
    for c in range(kv_dim // LANES):
        k = y[:, q_dim + c * LANES:q_dim + (c + 1) * LANES]
        if rope:
            k = _rope(k, cos, sin, take_upper)
        v = y[:, q_dim + kv_dim + c * LANES:q_dim + kv_dim + (c + 1) * LANES]
        kv_ref[0, :, c * LANES:(c + 1) * LANES] = k.astype(BF16)
        kv_ref[0, :, kv_dim + c * LANES:kv_dim + (c + 1) * LANES] = pltpu.roll(k, HEAD_DIM, axis=1).astype(BF16)
        kv_ref[0, :, 2 * kv_dim + c * LANES:2 * kv_dim + (c + 1) * LANES] = v.astype(BF16)
        kv_ref[0, :, 3 * kv_dim + c * LANES:3 * kv_dim + (c + 1) * LANES] = pltpu.roll(v, HEAD_DIM, axis=1).astype(BF16)


def _qkv_proj(x, norm_g, mods, layer, mod_row, w_qkv, slot, rope_tables):
    b, l, d = x.shape
    tm = min(512, l)
    rope = rope_tables is not None
    in_specs = [
        pl.BlockSpec((1, tm, d), lambda bi, i: (bi, i, 0)),
        _norm_spec(norm_g, layer, 2),
        _mod_spec(mods, layer, lambda bi, i: mod_row(bi)),
        _slot_spec(w_qkv, slot),
    ]
    args = [x, norm_g, mods, w_qkv]
    if rope:
        in_specs += [pl.BlockSpec((tm, LANES), lambda bi, i: (i, 0))] * 2
        args += list(rope_tables)
    q_dim = N_HEADS * HEAD_DIM
    return pl.pallas_call(
        functools.partial(_qkv_kernel, rope=rope),
        grid=(b, l // tm),
        in_specs=in_specs,
        out_specs=[pl.BlockSpec((1, tm, q_dim), lambda bi, i: (bi, i, 0)),
                   pl.BlockSpec((1, tm, q_dim), lambda bi, i: (bi, i, 0))],
        out_shape=[jax.ShapeDtypeStruct((b, l, q_dim), BF16),
                   jax.ShapeDtypeStruct((b, l, q_dim), BF16)],
        compiler_params=_cparams(2),
        name="qkv_rope" if rope else "qkv",
    )(*args)


def _rope_tables(n_tokens):
    t = jnp.arange(n_tokens)
    row = (t // GRID_W).astype(F32)
    col = (t % GRID_W).astype(F32)
    inv_freq = ROPE_BASE ** (-jnp.arange(ROPE_HALF, dtype=F32) / ROPE_HALF)
    ang_r = row[:, None] * inv_freq
    ang_c = col[:, None] * inv_freq
    cos_h = jnp.concatenate([jnp.cos(ang_r), jnp.cos(ang_r), jnp.cos(ang_c), jnp.cos(ang_c)], axis=-1)
    sin_h = jnp.concatenate([-jnp.sin(ang_r), jnp.sin(ang_r), -jnp.sin(ang_c), jnp.sin(ang_c)], axis=-1)
    reps = LANES // HEAD_DIM
    return jnp.tile(cos_h, (1, reps)), jnp.tile(sin_h, (1, reps))


def _pad_heads(dst_ref, rows, src):
    kv_dim = N_KV_HEADS * HEAD_DIM
    n = src.shape[0]
    lane = lax.broadcasted_iota(jnp.int32, (n, LANES), 1).astype(F32).astype(BF16)
    low = lane < HEAD_DIM
    zero = jnp.zeros((n, LANES), BF16)
    for t in range(2):
        for h in range(N_KV_HEADS):
            c = h // 2
            plain = src[:, 2 * t * kv_dim + c * LANES:2 * t * kv_dim + (c + 1) * LANES]
            swapped = src[:, (2 * t + 1) * kv_dim + c * LANES:(2 * t + 1) * kv_dim + (c + 1) * LANES]
            in_low, in_high = (plain, swapped) if h % 2 == 0 else (swapped, plain)
            base = t * 2 * N_KV_HEADS * LANES + h * 2 * LANES
            dst_ref[rows, base:base + LANES] = jnp.where(low, in_low, zero)
            dst_ref[rows, base + LANES:base + 2 * LANES] = jnp.where(low, zero, in_high)


def _lane_chunks(a):
    return [a[:, c * LANES:(c + 1) * LANES] for c in range(a.shape[1] // LANES)]


def _cast_spec(w, slot, n_steps, step_of):
    rows = w.shape[1] // n_steps
    assert rows * n_steps == w.shape[1] and rows % 16 == 0
    src = pl.BlockSpec((1, rows, w.shape[2]), lambda *idx: (slot, step_of(*idx), 0))
    dst = pl.BlockSpec((rows, w.shape[2]), lambda *idx: (step_of(*idx), 0))
    return src, dst, jax.ShapeDtypeStruct(w.shape[1:], BF16)


def _attn_kernel(*refs, tq, n_local_blocks, has_local, slot, side_cast):
    n_in = (6 if has_local else 3) + (1 if side_cast else 0)
    ins, rest = refs[:n_in], list(refs[n_in:])
    if has_local:
        sink_ref, q_ref, kvo_ref, kvp_ref, kvn_ref, kvc_ref = ins[:6]
    else:
        sink_ref, q_ref, kvc_ref = ins[:3]
    o_ref = rest.pop(0)
    if side_cast:
        rest.pop(0)[...] = ins[-1][0].astype(BF16)
    if has_local:
        loc_pad, ctx_pad = rest
    else:
        (ctx_pad,) = rest
    blk = ATTN_BLOCK
    i = pl.program_id(1)
    n_ctx = kvc_ref.shape[1]
    v_base = 2 * N_KV_HEADS * LANES
    _pad_heads(ctx_pad, slice(0, n_ctx), kvc_ref[0])
    if has_local:
        _pad_heads(loc_pad, slice(0, blk), kvp_ref[0])
        _pad_heads(loc_pad, slice(blk, blk + tq), kvo_ref[0])
        _pad_heads(loc_pad, slice(blk + tq, 2 * blk + tq), kvn_ref[0])

    lane_low = lax.broadcasted_iota(jnp.int32, (2 * blk, LANES), 1) < HEAD_DIM
    row_first = lax.broadcasted_iota(jnp.int32, (2 * blk, 1), 0) < blk

    def q_block(j, carry):
        r0 = pl.multiple_of(j * blk, blk)
        if has_local:
            n = i * (tq // blk) + j
            qi = lax.broadcasted_iota(jnp.int32, (2 * blk, 3 * blk), 0) & (blk - 1)
            ki = lax.broadcasted_iota(jnp.int32, (2 * blk, 3 * blk), 1)
            rel = ki - blk - qi
            k_lo = jnp.where(n == 0, blk, 0)
            k_hi = jnp.where(n == n_local_blocks - 1, 2 * blk, 3 * blk)
            valid = (rel >= -blk) & (rel <= blk) & (ki >= k_lo) & (ki < k_hi)
        for h in range(N_KV_HEADS):
            qp = q_ref[0, pl.ds(r0, blk), h * 2 * LANES:(h + 1) * 2 * LANES]
            lhs = jnp.concatenate([qp[:, :LANES], qp[:, LANES:]], axis=0)
            kc = h * 2 * LANES
            vc = v_base + h * 2 * LANES
            nt = (((1,), (1,)), ((), ()))
            seg_scores = [[], []]
            k_ctx = jnp.concatenate([ctx_pad[:, kc:kc + LANES], ctx_pad[:, kc + LANES:kc + 2 * LANES]], axis=0)
            s_ctx = lax.dot_general(lhs, k_ctx, nt, preferred_element_type=F32)
            seg_scores[0].append(s_ctx[:, :n_ctx])
            seg_scores[1].append(s_ctx[:, n_ctx:])
            if has_local:
                k_loc = jnp.concatenate([loc_pad[pl.ds(r0, 3 * blk), kc:kc + LANES],
                                         loc_pad[pl.ds(r0, 3 * blk), kc + LANES:kc + 2 * LANES]], axis=0)
                s_loc = lax.dot_general(lhs, k_loc, nt, preferred_element_type=F32)
                seg_scores[0].append(jnp.where(valid, s_loc[:, :3 * blk], NEG_INF))
                seg_scores[1].append(jnp.where(valid, s_loc[:, 3 * blk:], NEG_INF))
            probs, inv = [], []
            for seg in range(2):
                sink = jnp.where(row_first, sink_ref[slot, h * GROUP + seg],
                                 sink_ref[slot, h * GROUP + 2 + seg]) * LOG2_E
                col_max = functools.reduce(jnp.maximum, [c for s in seg_scores[seg] for c in _lane_chunks(s)])
                m = jnp.maximum(sink, jnp.max(col_max, axis=-1, keepdims=True))
                ps = [jnp.exp2(s - m) for s in seg_scores[seg]]
                col_sum = functools.reduce(jnp.add, [c for p in ps for c in _lane_chunks(p)])
                denom = jnp.exp2(sink - m) + jnp.sum(col_sum, axis=-1, keepdims=True)
                probs.append(ps)
                inv.append(1.0 / denom)
            p_ctx = jnp.concatenate([probs[0][0], probs[1][0]], axis=1).astype(BF16)
            v_ctx = jnp.concatenate([ctx_pad[:, vc:vc + LANES], ctx_pad[:, vc + LANES:vc + 2 * LANES]], axis=0)
            o = _dot(p_ctx, v_ctx)
            if has_local:
                p_loc = jnp.concatenate([probs[0][1], probs[1][1]], axis=1).astype(BF16)
                v_loc = jnp.concatenate([loc_pad[pl.ds(r0, 3 * blk), vc:vc + LANES],
                                         loc_pad[pl.ds(r0, 3 * blk), vc + LANES:vc + 2 * LANES]], axis=0)
                o = o + _dot(p_loc, v_loc)
            o = o * jnp.where(lane_low, inv[0], inv[1])
            o_ref[0, pl.ds(r0, blk), h * 2 * LANES:h * 2 * LANES + LANES] = o[:blk].astype(BF16)
            o_ref[0, pl.ds(r0, blk), h * 2 * LANES + LANES:(h + 1) * 2 * LANES] = o[blk:].astype(BF16)
        return carry

    lax.fori_loop(0, tq // blk, q_block, 0, unroll=2)


def _attention(q, kv, kv_ctx, sink, slot, has_local, cast=None):
    b, l, q_dim = q.shape
    n_ctx = kv_ctx.shape[1]
    blk = ATTN_BLOCK
    tq = min(1024, l)
    bpt = tq // blk
    n_blocks = l // blk
    pad_lanes = 4 * N_KV_HEADS * LANES
    in_specs = [pl.BlockSpec(memory_space=pltpu.SMEM),
                pl.BlockSpec((1, tq, q_dim), lambda bi, i: (bi, i, 0))]
    args = [sink, q]
    scratch = []
    if has_local:
        in_specs += [
            pl.BlockSpec((1, tq, q_dim), lambda bi, i: (bi, i, 0)),
            pl.BlockSpec((1, blk, q_dim), lambda bi, i: (bi, jnp.maximum(i * bpt - 1, 0), 0)),
            pl.BlockSpec((1, blk, q_dim), lambda bi, i: (bi, jnp.minimum((i + 1) * bpt, n_blocks - 1), 0)),
        ]
        args += [kv, kv, kv]
        scratch.append(pltpu.VMEM((tq + 2 * blk, pad_lanes), BF16))
    in_specs.append(pl.BlockSpec((1, n_ctx, q_dim), lambda bi, i: (bi, 0, 0)))
    args.append(kv_ctx)
    scratch.append(pltpu.VMEM((n_ctx, pad_lanes), BF16))
    out_specs = [pl.BlockSpec((1, tq, q_dim), lambda bi, i: (bi, i, 0))]
    out_shape = [jax.ShapeDtypeStruct((b, l, q_dim), BF16)]
    if cast is not None:
        n_i = l // tq
        src, dst, shape = _cast_spec(cast[0], cast[1], b * n_i, lambda bi, i: bi * n_i + i)
        in_specs.append(src)
        args.append(cast[0])
        out_specs.append(dst)
        out_shape.append(shape)
    out = pl.pallas_call(
        functools.partial(_attn_kernel, tq=tq, n_local_blocks=n_blocks, has_local=has_local, slot=slot,
                          side_cast=cast is not None),
        grid=(b, l // tq),
        in_specs=in_specs,
        out_specs=out_specs,
        out_shape=out_shape,
        scratch_shapes=scratch,
        compiler_params=_cparams(2),
        name="window_attention" if has_local else "context_attention",
    )(*args)
    return out if cast is not None else out[0]


def _oproj_ffn_kernel(a_ref, x_ref, g_ref, mod_ref, wo_ref, wgu_ref, wd_ref, *rest):
    if len(rest) == 3:
        cast_src, o_ref, cast_dst = rest
        cast_dst[...] = cast_src[0].astype(BF16)
    else:
        (o_ref,) = rest
    d_ff = wd_ref.shape[1]
    mod = lambda r: mod_ref[0, 0, r:r + 1, :]
    y = _dot(a_ref[0], wo_ref[0])
    x = x_ref[0] + mod(2) * _rmsnorm(y, g_ref[0, 1:2, :])
    h = _norm_mod(x, g_ref[0, 2:3, :], mod(3), mod(4)).astype(BF16)
    gate = _dot(h, wgu_ref[0, :, :d_ff])
    up = _dot(h, wgu_ref[0, :, d_ff:])
    f = _dot((_silu(gate) * up).astype(BF16), wd_ref[0])
    o_ref[0] = x + mod(5) * _rmsnorm(f, g_ref[0, 3:4, :])


def _oproj_ffn(attn, x, norm_g, mods, layer, mod_row, w_o, w_gu, w_down, slot, cast=None):
    b, l, d = x.shape
    tm = min(512, l)
    resident = pl.Buffered(1)
    in_specs = [
        pl.BlockSpec((1, tm, attn.shape[2]), lambda bi, i: (bi, i, 0)),
        pl.BlockSpec((1, tm, d), lambda bi, i: (bi, i, 0)),
        _norm_spec(norm_g, layer, 2),
        _mod_spec(mods, layer, lambda bi, i: mod_row(bi)),
        _slot_spec(w_o, slot, pipeline_mode=resident),
        _slot_spec(w_gu, slot, pipeline_mode=resident),
        _slot_spec(w_down, slot, pipeline_mode=resident),
    ]
    args = [attn, x, norm_g, mods, w_o, w_gu, w_down]
    out_specs = [pl.BlockSpec((1, tm, d), lambda bi, i: (bi, i, 0))]
    out_shape = [jax.ShapeDtypeStruct((b, l, d), F32)]
    if cast is not None:
        n_i = l // tm
        src, dst, shape = _cast_spec(cast[0], cast[1], b * n_i, lambda bi, i: bi * n_i + i)
        in_specs.append(src)
        args.append(cast[0])
        out_specs.append(dst)
        out_shape.append(shape)
    out = pl.pallas_call(
        _oproj_ffn_kernel,
        grid=(b, l // tm),
        in_specs=in_specs,
        out_specs=out_specs,
        out_shape=out_shape,
        compiler_params=_cparams(2),
        name="oproj_ffn",
    )(*args)
    return out if cast is not None else out[0]


def _pool_kernel(x_ref, xp_ref, xn_ref, g_ref, mod_ref, pw_ref, pb_ref, ps_ref, rw_ref, rb_ref,
                 xo_ref, h2_ref, route_ref, hbuf, ybuf, *, tm, seq_len):
    i = pl.program_id(1)
    n_blk = pl.num_programs(1)
    sub = 128
    halo = POOL_HALO
    gd = pw_ref.shape[2]
    mod = lambda r: mod_ref[0, 0, r:r + 1, :]
    g_pre, shift, scale = g_ref[0, 0:1, :], mod(0), mod(1)
    x = x_ref[0]
    hbuf[0:halo, :] = _norm_mod(xp_ref[0], g_pre, shift, scale) * jnp.where(i > 0, 1.0, 0.0)
    hbuf[halo:halo + tm, :] = _norm_mod(x, g_pre, shift, scale)
    hbuf[halo + tm:, :] = _norm_mod(xn_ref[0], g_pre, shift, scale) * jnp.where(i < n_blk - 1, 1.0, 0.0)

    r = lax.broadcasted_iota(jnp.int32, (sub, 2 * sub), 0)
    c = lax.broadcasted_iota(jnp.int32, (sub, 2 * sub), 1)
    t_row = lax.broadcasted_iota(jnp.int32, (sub, 1), 0)
    fill = jnp.zeros((sub - 2 * halo, gd), BF16)
    for gi, w in enumerate(POOL_WINDOWS):
        band = jnp.where((c >= r + halo - w // 2) & (c < r + halo - w // 2 + w), 1.0, 0.0).astype(BF16)
        for s in range(tm // sub):
            hs = hbuf[s * sub:s * sub + sub + 2 * halo, gi * gd:(gi + 1) * gd]
            hi = hs.astype(BF16)
            lo = (hs - hi.astype(F32)).astype(BF16)
            total = (_dot(band, jnp.concatenate([hi, fill], axis=0))
                     + _dot(band, jnp.concatenate([lo, fill], axis=0)))
            t = i * tm + s * sub + t_row
            count = jnp.clip(t - w // 2 + w, 0, seq_len) - jnp.clip(t - w // 2, 0, seq_len)
            centre = hbuf[halo + s * sub:halo + (s + 1) * sub, gi * gd:(gi + 1) * gd]
            diff = total * (1.0 / count.astype(F32)) - centre
            y = _dot(diff.astype(BF16), pw_ref[0, gi]) + pb_ref[0, :, gi * gd:(gi + 1) * gd]
            ybuf[s * sub:(s + 1) * sub, gi * gd:(gi + 1) * gd] = y * ps_ref[0, :, gi * gd:(gi + 1) * gd]

    xo = x + mod(2) * _rmsnorm(ybuf[...], g_ref[0, 1:2, :])
    xo_ref[0] = xo
    h2 = _norm_mod(xo, g_ref[0, 2:3, :], mod(3), mod(4))
    for s in range(SUBLANES):
        h2_ref[pl.ds(s, tm, stride=SUBLANES), :] = h2[:, s * LANES:(s + 1) * LANES]

    logits = _dot(h2.astype(BF16), rw_ref[0]) + rb_ref[0]
    lane = lax.broadcasted_iota(jnp.int32, logits.shape, 1)
    m1 = jnp.max(logits, axis=-1, keepdims=True)
    i1 = jnp.min(jnp.where(logits == m1, lane, LANES), axis=-1, keepdims=True)
    rest = jnp.where(lane == i1, NEG_INF * 2, logits)
    m2 = jnp.max(rest, axis=-1, keepdims=True)
    i2 = jnp.min(jnp.where(rest == m2, lane, LANES), axis=-1, keepdims=True)
    e = jnp.exp(m2 - m1)
    g1 = 1.0 / (1.0 + e)
    g2 = e / (1.0 + e)
    route_ref[0] = jnp.where(lane == 0, i1.astype(F32),
                             jnp.where(lane == 1, i2.astype(F32),
                                       jnp.where(lane == 2, g1, jnp.where(lane == 3, g2, 0.0))))


def _pool_mixer(x, norm_g, mods, layer, mod_row, pool_w, pool_b, pool_scale, router_w, router_b, slot):
    b, l, d = x.shape
    assert d == SUBLANES * LANES
    tm = min(512, l)
    hb = tm // POOL_HALO
    n_halo_blocks = l // POOL_HALO
    n_blk = l // tm
    row = lambda bi, i: (bi, i, 0)
    return pl.pallas_call(
        functools.partial(_pool_kernel, tm=tm, seq_len=l),
        grid=(b, n_blk),
        in_specs=[
            pl.BlockSpec((1, tm, d), row),
            pl.BlockSpec((1, POOL_HALO, d), lambda bi, i: (bi, jnp.maximum(i * hb - 1, 0), 0)),
            pl.BlockSpec((1, POOL_HALO, d), lambda bi, i: (bi, jnp.minimum((i + 1) * hb, n_halo_blocks - 1), 0)),
            _norm_spec(norm_g, layer, 2),
            _mod_spec(mods, layer, lambda bi, i: mod_row(bi)),
            _slot_spec(pool_w, slot),
            _slot_spec(pool_b, slot),
            _slot_spec(pool_scale, slot),
            _slot_spec(router_w, slot),
            _slot_spec(router_b, slot),
        ],
        out_specs=[pl.BlockSpec((1, tm, d), row),
                   pl.BlockSpec((tm * SUBLANES, LANES), lambda bi, i: (bi * n_blk + i, 0)),
                   pl.BlockSpec((1, tm, LANES), row)],
        out_shape=[jax.ShapeDtypeStruct((b, l, d), F32),
                   jax.ShapeDtypeStruct((b * l * SUBLANES, LANES), F32),
                   jax.ShapeDtypeStruct((b, l, LANES), F32)],
        scratch_shapes=[pltpu.VMEM((tm + 2 * POOL_HALO, d), F32), pltpu.VMEM((tm, d), F32)],
        compiler_params=_cparams(2),
        name="pool_mixer",
    )(x, x, x, norm_g, mods, pool_w, pool_b, pool_scale, router_w, router_b)


def _route_plan(route, tm):
    m = route.shape[0]
    n_tiles = (TOP_K * m) // tm + N_EXPERTS
    expert = route[:, :TOP_K].astype(jnp.int32).reshape(-1)
    onehot = (expert[:, None] == jnp.arange(N_EXPERTS, dtype=jnp.int32)[None, :]).astype(jnp.int32)
    csum = jnp.cumsum(onehot, axis=0)
    counts = csum[-1]
    rank = jnp.sum(csum * onehot, axis=1) - 1
    padded = ((counts + tm - 1) // tm) * tm
    ends = jnp.cumsum(padded)
    starts = ends - padded
    pos = jnp.sum(starts[None, :] * onehot, axis=1) + rank
    tile_start = jnp.arange(n_tiles, dtype=jnp.int32) * tm
    tile_expert = jnp.minimum(jnp.sum((tile_start[:, None] >= ends[None, :]).astype(jnp.int32), axis=1),
                              N_EXPERTS - 1)
    n_used = (ends[-1] // tm).reshape(1)
    i32 = lambda a: a.astype(jnp.int32)
    return i32(pos), i32(tile_expert), i32(n_used), i32(jnp.concatenate([starts, ends, n_used]))


def _tile(ref, row):
    return ref.at[pl.ds(pl.multiple_of(row * SUBLANES, SUBLANES), SUBLANES)]


def _wait_tiles(n, src_ref, dst_ref, sem):
    def wait(r, carry):
        pltpu.make_async_copy(_tile(src_ref, 0), _tile(dst_ref, 0), sem).wait()
        return carry
    lax.fori_loop(0, n, wait, 0, unroll=8)


def _dispatch_kernel(pos_ref, seg_ref, h_ref, xs_hbm, zeros_ref, sem, *, tc, tm):
    i = pl.program_id(0)
    zero_rows = zeros_ref.shape[0] // SUBLANES
    n_tiles = xs_hbm.shape[0] // (tm * SUBLANES)

    def clear_tile(first_row):
        for c in range(tm // zero_rows):
            first = pl.multiple_of((first_row + c * zero_rows) * SUBLANES, SUBLANES)
            cp = pltpu.make_async_copy(zeros_ref, xs_hbm.at[pl.ds(first, zero_rows * SUBLANES)], sem)
            cp.start()
            cp.wait()

    @pl.when(i == 0)
    def _():
        zeros_ref[...] = jnp.zeros_like(zeros_ref)
        for e in range(N_EXPERTS):
            start, end = seg_ref[e], seg_ref[N_EXPERTS + e]
            pl.when(end > start)(functools.partial(clear_tile, end - tm))
            spare = seg_ref[2 * N_EXPERTS] + e
            pl.when(spare < n_tiles)(functools.partial(clear_tile, spare * tm))

    base = i * tc * TOP_K

    def issue(r, carry):
        for k in range(TOP_K):
            p = pos_ref[base + TOP_K * r + k]
            pltpu.make_async_copy(_tile(h_ref, r), _tile(xs_hbm, p), sem).start(priority=k)
        return carry
    lax.fori_loop(0, tc, issue, 0, unroll=8)
    _wait_tiles(TOP_K * tc, h_ref, xs_hbm, sem)


def _dispatch(h2t, pos, seg, n_rows, tm):
    m = h2t.shape[0] // SUBLANES
    tc = min(512, m)
    return pl.pallas_call(
        functools.partial(_dispatch_kernel, tc=tc, tm=tm),
        grid_spec=pltpu.PrefetchScalarGridSpec(
            num_scalar_prefetch=2,
            grid=(m // tc,),
            in_specs=[pl.BlockSpec((tc * SUBLANES, LANES), lambda i, pos, seg: (i, 0))],
            out_specs=pl.BlockSpec(memory_space=pl.ANY),
            scratch_shapes=[pltpu.VMEM((min(tm, 256) * SUBLANES, LANES), F32), pltpu.SemaphoreType.DMA],
        ),
        out_shape=jax.ShapeDtypeStruct((n_rows * SUBLANES, LANES), F32),
        compiler_params=_cparams(1),
        name="moe_dispatch",
    )(pos, seg, h2t)


def _experts_kernel(te_ref, nused_ref, xs_ref, wgu_ref, wd_ref, ys_ref, *, tm, n_chunks):
    i = pl.program_id(0)
    d_ff = wd_ref.shape[1]
    tf = d_ff // n_chunks

    @pl.when(i < nused_ref[0])
    def _():
        x = jnp.concatenate([xs_ref[pl.ds(s, tm, stride=SUBLANES), :].astype(BF16)
                             for s in range(SUBLANES)], axis=1)
        acc = None
        for c in range(n_chunks):
            gate = _dot(x, wgu_ref[0, :, c * tf:(c + 1) * tf])
            up = _dot(x, wgu_ref[0, :, d_ff + c * tf:d_ff + (c + 1) * tf])
            part = _dot((_silu(gate) * up).astype(BF16), wd_ref[0, c * tf:(c + 1) * tf, :])
            acc = part if acc is None else acc + part
        for s in range(SUBLANES):
            ys_ref[pl.ds(s, tm, stride=SUBLANES), :] = acc[:, s * LANES:(s + 1) * LANES]

    @pl.when(i >= nused_ref[0])
    def _():
        ys_ref[...] = jnp.zeros_like(ys_ref)


def _experts(xs, tile_expert, n_used, w_gu, w_down, tm):
    n_rows = xs.shape[0] // SUBLANES
    resident = pl.Buffered(1)
    return pl.pallas_call(
        functools.partial(_experts_kernel, tm=tm, n_chunks=2),
        grid_spec=pltpu.PrefetchScalarGridSpec(
            num_scalar_prefetch=2,
            grid=(n_rows // tm,),
            in_specs=[
                pl.BlockSpec((tm * SUBLANES, LANES), lambda i, te, nu: (i, 0)),
                pl.BlockSpec((1,) + w_gu.shape[1:], lambda i, te, nu: (te[i], 0, 0), pipeline_mode=resident),
                pl.BlockSpec((1,) + w_down.shape[1:], lambda i, te, nu: (te[i], 0, 0)),
            ],
            out_specs=pl.BlockSpec((tm * SUBLANES, LANES), lambda i, te, nu: (i, 0)),
        ),
        out_shape=jax.ShapeDtypeStruct((n_rows * SUBLANES, LANES), F32),
        compiler_params=_cparams(1),
        name="moe_experts",
    )(tile_expert, n_used, xs, w_gu, w_down)


def _combine_kernel(pos_ref, ys_hbm, route_ref, x_ref, g_ref, mod_ref, o_ref, buf, sem, *, tc):
    i = pl.program_id(0)
    n = pl.num_programs(0)

    def gather(block, slot):
        base = block * tc * TOP_K

        def issue(r, carry):
            for k in range(TOP_K):
                p = pos_ref[base + TOP_K * r + k]
                pltpu.make_async_copy(_tile(ys_hbm, p), _tile(buf.at[slot, k], r), sem.at[slot]).start(priority=k)
            return carry
        lax.fori_loop(0, tc, issue, 0, unroll=8)

    @pl.when(i == 0)
    def _():
        gather(0, 0)

    @pl.when(i + 1 < n)
    def _():
        gather(i + 1, (i + 1) % 2)

    slot = i % 2
    _wait_tiles(TOP_K * tc, ys_hbm, buf.at[slot, 0], sem.at[slot])
    rows = lambda k: jnp.concatenate([buf[slot, k, pl.ds(s, tc, stride=SUBLANES), :] for s in range(SUBLANES)], axis=1)
    route = route_ref[...]
    f = route[:, 2:3] * rows(0) + route[:, 3:4] * rows(1)
    o_ref[...] = x_ref[...] + mod_ref[0, 0, 5:6, :] * _rmsnorm(f, g_ref[0, 3:4, :])


def _combine(ys, pos, route, x, norm_g, mods, layer, mod_row_of_block, tc):
    m, d = x.shape
    return pl.pallas_call(
        functools.partial(_combine_kernel, tc=tc),
        grid_spec=pltpu.PrefetchScalarGridSpec(
            num_scalar_prefetch=1,
            grid=(m // tc,),
            in_specs=[
                pl.BlockSpec(memory_space=pl.ANY),
                pl.BlockSpec((tc, LANES), lambda i, pos: (i, 0)),
                pl.BlockSpec((tc, d), lambda i, pos: (i, 0)),
                _norm_spec(norm_g, layer, 1),
                _mod_spec(mods, layer, lambda i, pos: mod_row_of_block(i)),
            ],
            out_specs=pl.BlockSpec((tc, d), lambda i, pos: (i, 0)),
            scratch_shapes=[pltpu.VMEM((2, TOP_K, tc * SUBLANES, LANES), F32),
                            pltpu.SemaphoreType.DMA((2,))],
        ),
        out_shape=jax.ShapeDtypeStruct((m, d), F32),
        compiler_params=_cparams(1),
        name="moe_combine",
    )(pos, ys, route, x, norm_g, mods)


def _moe(h2t, route, x, norm_g, mods, layer, mod_row, w_gu, w_down):
    b, l, d = x.shape
    m = b * l
    tm = min(512, l)
    tc = min(256, l)
    route2 = route.reshape(m, LANES)
    pos, tile_expert, n_used, seg = _route_plan(route2, tm)
    n_rows = TOP_K * m + N_EXPERTS * tm
    xs = _dispatch(h2t, pos, seg, n_rows, tm)
    ys = _experts(xs, tile_expert, n_used, w_gu, w_down, tm)
    blocks_per_seq = l // tc
    out = _combine(ys, pos, route2, x.reshape(m, d), norm_g, mods, layer,
                   lambda i: mod_row(i // blocks_per_seq), tc)
    return out.reshape(b, l, d)


def kernel(x, c, ctx, c_ctx, ada_w, ada_b, norm_g, attn_w_qkv, attn_w_o, attn_sink, pool_w, pool_b,
           pool_scale, ffn_w_gu, ffn_w_down, router_w, router_b, moe_w_gu, moe_w_down):
    batch, seq, d = x.shape
    depth = ada_w.shape[0]
    ctx_row = batch
    assert batch < MOD_ROWS and seq % ATTN_BLOCK == 0 and d == N_HEADS * HEAD_DIM

    cond = jnp.zeros((MOD_ROWS, d), F32).at[:batch].set(c).at[ctx_row].set(c_ctx)
    mods = _ada_mods(cond, ada_w, ada_b).reshape(depth, MOD_ROWS, 6, d)
    latent_row = lambda bi: bi
    context_row = lambda bi: ctx_row
    rope = _rope_tables(seq)

    w_qkv = attn_w_qkv.astype(BF16)
    w_o = attn_w_o.astype(BF16)
    w_gu = ffn_w_gu.astype(BF16)
    w_down = ffn_w_down.astype(BF16)
    p_w = pool_w.astype(BF16)
    p_b = pool_b.reshape(-1, 1, d)
    p_s = pool_scale.reshape(-1, 1, d)
    n_moe, n_exp, _, gu_cols = moe_w_gu.shape
    d_ff_e = moe_w_down.shape[2]
    gu_src = moe_w_gu.reshape(n_moe, n_exp * d, gu_cols)
    down_src = moe_w_down.reshape(n_moe, n_exp * d_ff_e, d)
    experts_bf16 = {}
    r_w = jnp.zeros((router_w.shape[0], d, LANES), BF16).at[:, :, :n_exp].set(router_w.astype(BF16))
    r_b = jnp.full((router_b.shape[0], 1, LANES), NEG_INF, F32).at[:, 0, :n_exp].set(router_b)

    is_attn = [i % 2 == 0 for i in range(depth)]
    for i in range(depth):
        slot = i // 2
        ctx_later = any(is_attn[i + 1:])
        if is_attn[i]:
            q, kv = _qkv_proj(x, norm_g, mods, i, latent_row, w_qkv, slot, rope)
            qc, kvc = _qkv_proj(ctx, norm_g, mods, i, context_row, w_qkv, slot, None)
            if i + 1 < depth and not is_attn[i + 1]:
                moe_slot = (i + 1) // 2
                attn, gu = _attention(q, kv, kvc, attn_sink, slot, True, cast=(gu_src, moe_slot))
                x, down = _oproj_ffn(attn, x, norm_g, mods, i, latent_row, w_o, w_gu, w_down, slot,
                                     cast=(down_src, moe_slot))
                experts_bf16[moe_slot] = (gu.reshape(n_exp, d, gu_cols), down.reshape(n_exp, d_ff_e, d))
            else:
                attn = _attention(q, kv, kvc, attn_sink, slot, True)
                x = _oproj_ffn(attn, x, norm_g, mods, i, latent_row, w_o, w_gu, w_down, slot)
            if ctx_later:
                attn_c = _attention(qc, None, kvc, attn_sink, slot, False)
                ctx = _oproj_ffn(attn_c, ctx, norm_g, mods, i, context_row, w_o, w_gu, w_down, slot)
        else:
            if slot not in experts_bf16:
                experts_bf16[slot] = (moe_w_gu[slot].astype(BF16), moe_w_down[slot].astype(BF16))
            e_gu, e_down = experts_bf16[slot]
            x1, h2t, route = _pool_mixer(x, norm_g, mods, i, latent_row, p_w, p_b, p_s, r_w, r_b, slot)
            x = _moe(h2t, route, x1, norm_g, mods, i, latent_row, e_gu, e_down)
            if ctx_later:
                c1, h2c, route_c = _pool_mixer(ctx, norm_g, mods, i, context_row, p_w, p_b, p_s, r_w, r_b, slot)
                ctx = _moe(h2c, route_c, c1, norm_g, mods, i, context_row, e_gu, e_down)
    return x
```

```python
import functools

import jax
import jax.numpy as jnp
from jax import lax
from jax.experimental import pallas as pl
from jax.experimental.pallas import tpu as pltpu

F32 = jnp.float32
BF16 = jnp.bfloat16

N_HEADS = 16
N_KV_HEADS = 4
HEAD_DIM = 64
GROUP = N_HEADS // N_KV_HEADS
ATTN_BLOCK = 128
GRID_W = 64
ROPE_BASE = 10000.0
ROPE_HALF = HEAD_DIM // 4
POOL_WINDOWS = (2, 4, 8, 16)
POOL_HALO = 8
N_EXPERTS = 8
TOP_K = 2
NORM_EPS = 1e-6
NEG_INF = -1e30
LOG2_E = 1.4426950408889634
Q_SCALE = HEAD_DIM ** -0.5 * LOG2_E
LANES = 128
SUBLANES = 8
MOD_ROWS = 8
VMEM_LIMIT = 56 * 1024 * 1024


def _cparams(n_axes):
    return pltpu.CompilerParams(dimension_semantics=("arbitrary",) * n_axes,
                                vmem_limit_bytes=VMEM_LIMIT)


def _rmsnorm(x, g):
    return x * lax.rsqrt(jnp.mean(x * x, axis=-1, keepdims=True) + NORM_EPS) * g


def _norm_mod(x, g, shift, scale):
    return _rmsnorm(x, g) * (1.0 + scale) + shift


def _silu(x):
    return x / (1.0 + jnp.exp(-x))


def _dot(a, b):
    return jnp.dot(a, b, preferred_element_type=F32)


def _norm_spec(norm_g, layer, n_grid):
    return pl.BlockSpec((1,) + norm_g.shape[1:], lambda *_: (layer, 0, 0))


def _mod_spec(mods, layer, row_of):
    return pl.BlockSpec((1, 1) + mods.shape[2:], lambda *idx: (layer, row_of(*idx), 0, 0))


def _slot_spec(w, slot, **kw):
    zeros = (0,) * (w.ndim - 1)
    return pl.BlockSpec((1,) + w.shape[1:], lambda *_: (slot,) + zeros, **kw)


def _ada_kernel(cond_ref, w_ref, b_ref, o_ref):
    s = _silu(cond_ref[...]).astype(BF16)
    o_ref[0] = _dot(s, w_ref[0].astype(BF16)) + b_ref[0]


def _ada_mods(cond, ada_w, ada_b):
    depth, d, n = ada_w.shape
    tn = 1536
    return pl.pallas_call(
        _ada_kernel,
        grid=(depth, n // tn),
        in_specs=[
            pl.BlockSpec((MOD_ROWS, d), lambda l, j: (0, 0)),
            pl.BlockSpec((1, d, tn), lambda l, j: (l, 0, j)),
            pl.BlockSpec((1, 1, tn), lambda l, j: (l, 0, j)),
        ],
        out_specs=pl.BlockSpec((1, MOD_ROWS, tn), lambda l, j: (l, 0, j)),
        out_shape=jax.ShapeDtypeStruct((depth, MOD_ROWS, n), F32),
        compiler_params=_cparams(2),
        name="ada_mods",
    )(cond, ada_w, ada_b.reshape(depth, 1, n))


def _rope(chunk, cos, sin, take_upper):
    rot = jnp.where(take_upper, pltpu.roll(chunk, LANES - ROPE_HALF, axis=1),
                    pltpu.roll(chunk, ROPE_HALF, axis=1))
    return chunk * cos + rot * sin


def _qkv_kernel(*refs, rope):
    if rope:
        x_ref, g_ref, mod_ref, w_ref, cos_ref, sin_ref, q_ref, kv_ref = refs
    else:
        x_ref, g_ref, mod_ref, w_ref, q_ref, kv_ref = refs
    h = _norm_mod(x_ref[0], g_ref[0, 0:1, :], mod_ref[0, 0, 0:1, :], mod_ref[0, 0, 1:2, :])
    y = _dot(h.astype(BF16), w_ref[0])
    q_dim = N_HEADS * HEAD_DIM
    kv_dim = N_KV_HEADS * HEAD_DIM
    if rope:
        cos, sin = cos_ref[...], sin_ref[...]
        lane = lax.broadcasted_iota(jnp.int32, cos.shape, 1)
        take_upper = (lane % (2 * ROPE_HALF)) < ROPE_HALF
    for c in range(q_dim // LANES):
        chunk = y[:, c * LANES:(c + 1) * LANES]
        if rope:
            chunk = _rope(chunk, cos, sin, take_upper)
        q_ref[0, :, c * LANES:(c + 1) * LANES] = (chunk * Q_SCALE).astype(BF16)
    for c in range(kv_dim // LANES):
        k = y[:, q_dim + c * LANES:q_dim + (c + 1) * LANES]
        if rope:
            k = _rope(k, cos, sin, take_upper)
        v = y[:, q_dim + kv_dim + c * LANES:q_dim + kv_dim + (c + 1) * LANES]
        kv_ref[0, :, c * LANES:(c + 1) * LANES] = k.astype(BF16)
        kv_ref[0, :, kv_dim + c * LANES:kv_dim + (c + 1) * LANES] = pltpu.roll(k, HEAD_DIM, axis=1).astype(BF16)
        kv_ref[0, :, 2 * kv_dim + c * LANES:2 * kv_dim + (c + 1) * LANES] = v.astype(BF16)
        kv_ref[0, :, 3 * kv_dim + c * LANES:3 * kv_dim + (c + 1) * LANES] = pltpu.roll(v, HEAD_DIM, axis=1).astype(BF16)


def _qkv_proj(x, norm_g, mods, layer, mod_row, w_qkv, slot, rope_tables):
    b, l, d = x.shape
    tm = min(512, l)
    rope = rope_tables is not None
    in_specs = [
        pl.BlockSpec((1, tm, d), lambda bi, i: (bi, i, 0)),
        _norm_spec(norm_g, layer, 2),
        _mod_spec(mods, layer, lambda bi, i: mod_row(bi)),
        _slot_spec(w_qkv, slot),
    ]
    args = [x, norm_g, mods, w_qkv]
    if rope:
        in_specs += [pl.BlockSpec((tm, LANES), lambda bi, i: (i, 0))] * 2
        args += list(rope_tables)
    q_dim = N_HEADS * HEAD_DIM
    return pl.pallas_call(
        functools.partial(_qkv_kernel, rope=rope),
        grid=(b, l // tm),
        in_specs=in_specs,
        out_specs=[pl.BlockSpec((1, tm, q_dim), lambda bi, i: (bi, i, 0)),
                   pl.BlockSpec((1, tm, q_dim), lambda bi, i: (bi, i, 0))],
        out_shape=[jax.ShapeDtypeStruct((b, l, q_dim), BF16),
                   jax.ShapeDtypeStruct((b, l, q_dim), BF16)],
        compiler_params=_cparams(2),
        name="qkv_rope" if rope else "qkv",
    )(*args)


def _rope_tables(n_tokens):
    t = jnp.arange(n_tokens)
    row = (t // GRID_W).astype(F32)
    col = (t % GRID_W).astype(F32)
    inv_freq = ROPE_BASE ** (-jnp.arange(ROPE_HALF, dtype=F32) / ROPE_HALF)
    ang_r = row[:, None] * inv_freq
    ang_c = col[:, None] * inv_freq
    cos_h = jnp.concatenate([jnp.cos(ang_r), jnp.cos(ang_r), jnp.cos(ang_c), jnp.cos(ang_c)], axis=-1)
    sin_h = jnp.concatenate([-jnp.sin(ang_r), jnp.sin(ang_r), -jnp.sin(ang_c), jnp.sin(ang_c)], axis=-1)
    reps = LANES // HEAD_DIM
    return jnp.tile(cos_h, (1, reps)), jnp.tile(sin_h, (1, reps))


def _pad_heads(dst_ref, rows, src):
    kv_dim = N_KV_HEADS * HEAD_DIM
    n = src.shape[0]
    lane = lax.broadcasted_iota(jnp.int32, (n, LANES), 1).astype(F32).astype(BF16)
    low = lane < HEAD_DIM
    zero = jnp.zeros((n, LANES), BF16)
    for t in range(2):
        for h in range(N_KV_HEADS):
            c = h // 2
            plain = src[:, 2 * t * kv_dim + c * LANES:2 * t * kv_dim + (c + 1) * LANES]
            swapped = src[:, (2 * t + 1) * kv_dim + c * LANES:(2 * t + 1) * kv_dim + (c + 1) * LANES]
            in_low, in_high = (plain, swapped) if h % 2 == 0 else (swapped, plain)
            base = t * 2 * N_KV_HEADS * LANES + h * 2 * LANES
            dst_ref[rows, base:base + LANES] = jnp.where(low, in_low, zero)
            dst_ref[rows, base + LANES:base + 2 * LANES] = jnp.where(low, zero, in_high)


def _lane_chunks(a):
    return [a[:, c * LANES:(c + 1) * LANES] for c in range(a.shape[1] // LANES)]


def _cast_spec(w, slot, n_steps, step_of):
    rows = w.shape[1] // n_steps
    assert rows * n_steps == w.shape[1] and rows % 16 == 0
    src = pl.BlockSpec((1, rows, w.shape[2]), lambda *idx: (slot, step_of(*idx), 0))
    dst = pl.BlockSpec((rows, w.shape[2]), lambda *idx: (step_of(*idx), 0))
    return src, dst, jax.ShapeDtypeStruct(w.shape[1:], BF16)


def _attn_kernel(*refs, tq, n_local_blocks, has_local, slot, side_cast):
    n_in = (6 if has_local else 3) + (1 if side_cast else 0)
    ins, rest = refs[:n_in], list(refs[n_in:])
    if has_local:
        sink_ref, q_ref, kvo_ref, kvp_ref, kvn_ref, kvc_ref = ins[:6]
    else:
        sink_ref, q_ref, kvc_ref = ins[:3]
    o_ref = rest.pop(0)
    if side_cast:
        rest.pop(0)[...] = ins[-1][0].astype(BF16)
    if has_local:
        loc_pad, ctx_pad = rest
    else:
        (ctx_pad,) = rest
    blk = ATTN_BLOCK
    i = pl.program_id(1)
    n_ctx = kvc_ref.shape[1]
    v_base = 2 * N_KV_HEADS * LANES
    _pad_heads(ctx_pad, slice(0, n_ctx), kvc_ref[0])
    if has_local:
        _pad_heads(loc_pad, slice(0, blk), kvp_ref[0])
        _pad_heads(loc_pad, slice(blk, blk + tq), kvo_ref[0])
        _pad_heads(loc_pad, slice(blk + tq, 2 * blk + tq), kvn_ref[0])

    lane_low = lax.broadcasted_iota(jnp.int32, (2 * blk, LANES), 1) < HEAD_DIM
    row_first = lax.broadcasted_iota(jnp.int32, (2 * blk, 1), 0) < blk

    def q_block(j, carry):
        r0 = pl.multiple_of(j * blk, blk)
        if has_local:
            n = i * (tq // blk) + j
            qi = lax.broadcasted_iota(jnp.int32, (2 * blk, 3 * blk), 0) & (blk - 1)
            ki = lax.broadcasted_iota(jnp.int32, (2 * blk, 3 * blk), 1)
            rel = ki - blk - qi
            k_lo = jnp.where(n == 0, blk, 0)
            k_hi = jnp.where(n == n_local_blocks - 1, 2 * blk, 3 * blk)
            valid = (rel >= -blk) & (rel <= blk) & (ki >= k_lo) & (ki < k_hi)
        for h in range(N_KV_HEADS):
            qp = q_ref[0, pl.ds(r0, blk), h * 2 * LANES:(h + 1) * 2 * LANES]
            lhs = jnp.concatenate([qp[:, :LANES], qp[:, LANES:]], axis=0)
            kc = h * 2 * LANES
            vc = v_base + h * 2 * LANES
            nt = (((1,), (1,)), ((), ()))
            seg_scores = [[], []]
            k_ctx = jnp.concatenate([ctx_pad[:, kc:kc + LANES], ctx_pad[:, kc + LANES:kc + 2 * LANES]], axis=0)
            s_ctx = lax.dot_general(lhs, k_ctx, nt, preferred_element_type=F32)
            seg_scores[0].append(s_ctx[:, :n_ctx])
            seg_scores[1].append(s_ctx[:, n_ctx:])
            if has_local:
                k_loc = jnp.concatenate([loc_pad[pl.ds(r0, 3 * blk), kc:kc + LANES],
                                         loc_pad[pl.ds(r0, 3 * blk), kc + LANES:kc + 2 * LANES]], axis=0)
                s_loc = lax.dot_general(lhs, k_loc, nt, preferred_element_type=F32)
                seg_scores[0].append(jnp.where(valid, s_loc[:, :3 * blk], NEG_INF))
                seg_scores[1].append(jnp.where(valid, s_loc[:, 3 * blk:], NEG_INF))
            probs, inv = [], []
            for seg in range(2):
                sink = jnp.where(row_first, sink_ref[slot, h * GROUP + seg],
                                 sink_ref[slot, h * GROUP + 2 + seg]) * LOG2_E
                col_max = functools.reduce(jnp.maximum, [c for s in seg_scores[seg] for c in _lane_chunks(s)])
                m = jnp.maximum(sink, jnp.max(col_max, axis=-1, keepdims=True))
                ps = [jnp.exp2(s - m) for s in seg_scores[seg]]
                col_sum = functools.reduce(jnp.add, [c for p in ps for c in _lane_chunks(p)])
                denom = jnp.exp2(sink - m) + jnp.sum(col_sum, axis=-1, keepdims=True)
                probs.append(ps)
                inv.append(1.0 / denom)
            p_ctx = jnp.concatenate([probs[0][0], probs[1][0]], axis=1).astype(BF16)
            v_ctx = jnp.concatenate([ctx_pad[:, vc:vc + LANES], ctx_pad[:, vc + LANES:vc + 2 * LANES]], axis=0)
            o = _dot(p_ctx, v_ctx)
            if has_local:
                p_loc = jnp.concatenate([probs[0][1], probs[1][1]], axis=1).astype(BF16)
                v_loc = jnp.concatenate([loc_pad[pl.ds(r0, 3 * blk), vc:vc + LANES],
                                         loc_pad[pl.ds(r0, 3 * blk), vc + LANES:vc + 2 * LANES]], axis=0)
                o = o + _dot(p_loc, v_loc)
            o = o * jnp.where(lane_low, inv[0], inv[1])
            o_ref[0, pl.ds(r0, blk), h * 2 * LANES:h * 2 * LANES + LANES] = o[:blk].astype(BF16)
            o_ref[0, pl.ds(r0, blk), h * 2 * LANES + LANES:(h + 1) * 2 * LANES] = o[blk:].astype(BF16)
        return carry

    lax.fori_loop(0, tq // blk, q_block, 0, unroll=2)


def _attention(q, kv, kv_ctx, sink, slot, has_local, cast=None):
    b, l, q_dim = q.shape
    n_ctx = kv_ctx.shape[1]
    blk = ATTN_BLOCK
    tq = min(1024, l)
    bpt = tq // blk
    n_blocks = l // blk
    pad_lanes = 4 * N_KV_HEADS * LANES
    in_specs = [pl.BlockSpec(memory_space=pltpu.SMEM),
                pl.BlockSpec((1, tq, q_dim), lambda bi, i: (bi, i, 0))]
    args = [sink, q]
    scratch = []
    if has_local:
        in_specs += [
            pl.BlockSpec((1, tq, q_dim), lambda bi, i: (bi, i, 0)),
            pl.BlockSpec((1, blk, q_dim), lambda bi, i: (bi, jnp.maximum(i * bpt - 1, 0), 0)),
            pl.BlockSpec((1, blk, q_dim), lambda bi, i: (bi, jnp.minimum((i + 1) * bpt, n_blocks - 1), 0)),
        ]
        args += [kv, kv, kv]
        scratch.append(pltpu.VMEM((tq + 2 * blk, pad_lanes), BF16))
    in_specs.append(pl.BlockSpec((1, n_ctx, q_dim), lambda bi, i: (bi, 0, 0)))
    args.append(kv_ctx)
    scratch.append(pltpu.VMEM((n_ctx, pad_lanes), BF16))
    out_specs = [pl.BlockSpec((1, tq, q_dim), lambda bi, i: (bi, i, 0))]
    out_shape = [jax.ShapeDtypeStruct((b, l, q_dim), BF16)]
    if cast is not None:
        n_i = l // tq
        src, dst, shape = _cast_spec(cast[0], cast[1], b * n_i, lambda bi, i: bi * n_i + i)
        in_specs.append(src)
        args.append(cast[0])
        out_specs.append(dst)
        out_shape.append(shape)
    out = pl.pallas_call(
        functools.partial(_attn_kernel, tq=tq, n_local_blocks=n_blocks, has_local=has_local, slot=slot,
                          side_cast=cast is not None),
        grid=(b, l // tq),
        in_specs=in_specs,
        out_specs=out_specs,
        out_shape=out_shape,
        scratch_shapes=scratch,
        compiler_params=_cparams(2),
        name="window_attention" if has_local else "context_attention",
    )(*args)
    return out if cast is not None else out[0]


def _oproj_ffn_kernel(a_ref, x_ref, g_ref, mod_ref, wo_ref, wgu_ref, wd_ref, *rest):
    if len(rest) == 3:
        cast_src, o_ref, cast_dst = rest
        cast_dst[...] = cast_src[0].astype(BF16)
    else:
        (o_ref,) = rest
    d_ff = wd_ref.shape[1]
    mod = lambda r: mod_ref[0, 0, r:r + 1, :]
    y = _dot(a_ref[0], wo_ref[0])
    x = x_ref[0] + mod(2) * _rmsnorm(y, g_ref[0, 1:2, :])
    h = _norm_mod(x, g_ref[0, 2:3, :], mod(3), mod(4)).astype(BF16)
    gate = _dot(h, wgu_ref[0, :, :d_ff])
    up = _dot(h, wgu_ref[0, :, d_ff:])
    f = _dot((_silu(gate) * up).astype(BF16), wd_ref[0])
    o_ref[0] = x + mod(5) * _rmsnorm(f, g_ref[0, 3:4, :])


def _oproj_ffn(attn, x, norm_g, mods, layer, mod_row, w_o, w_gu, w_down, slot, cast=None):
    b, l, d = x.shape
    tm = min(512, l)
    resident = pl.Buffered(1)
    in_specs = [
        pl.BlockSpec((1, tm, attn.shape[2]), lambda bi, i: (bi, i, 0)),
        pl.BlockSpec((1, tm, d), lambda bi, i: (bi, i, 0)),
        _norm_spec(norm_g, layer, 2),
        _mod_spec(mods, layer, lambda bi, i: mod_row(bi)),
        _slot_spec(w_o, slot, pipeline_mode=resident),
        _slot_spec(w_gu, slot, pipeline_mode=resident),
        _slot_spec(w_down, slot, pipeline_mode=resident),
    ]
    args = [attn, x, norm_g, mods, w_o, w_gu, w_down]
    out_specs = [pl.BlockSpec((1, tm, d), lambda bi, i: (bi, i, 0))]
    out_shape = [jax.ShapeDtypeStruct((b, l, d), F32)]
    if cast is not None:
        n_i = l // tm
        src, dst, shape = _cast_spec(cast[0], cast[1], b * n_i, lambda bi, i: bi * n_i + i)
        in_specs.append(src)
        args.append(cast[0])
        out_specs.append(dst)
        out_shape.append(shape)
    out = pl.pallas_call(
        _oproj_ffn_kernel,
        grid=(b, l // tm),
        in_specs=in_specs,
        out_specs=out_specs,
        out_shape=out_shape,
        compiler_params=_cparams(2),
        name="oproj_ffn",
    )(*args)
    return out if cast is not None else out[0]


def _pool_kernel(x_ref, xp_ref, xn_ref, g_ref, mod_ref, pw_ref, pb_ref, ps_ref, rw_ref, rb_ref,
                 xo_ref, h2_ref, route_ref, hbuf, ybuf, *, tm, seq_len):
    i = pl.program_id(1)
    n_blk = pl.num_programs(1)
    sub = 128
    halo = POOL_HALO
    gd = pw_ref.shape[2]
    mod = lambda r: mod_ref[0, 0, r:r + 1, :]
    g_pre, shift, scale = g_ref[0, 0:1, :], mod(0), mod(1)
    x = x_ref[0]
    hbuf[0:halo, :] = _norm_mod(xp_ref[0], g_pre, shift, scale) * jnp.where(i > 0, 1.0, 0.0)
    hbuf[halo:halo + tm, :] = _norm_mod(x, g_pre, shift, scale)
    hbuf[halo + tm:, :] = _norm_mod(xn_ref[0], g_pre, shift, scale) * jnp.where(i < n_blk - 1, 1.0, 0.0)

    r = lax.broadcasted_iota(jnp.int32, (sub, 2 * sub), 0)
    c = lax.broadcasted_iota(jnp.int32, (sub, 2 * sub), 1)
    t_row = lax.broadcasted_iota(jnp.int32, (sub, 1), 0)
    fill = jnp.zeros((sub - 2 * halo, gd), BF16)
    for gi, w in enumerate(POOL_WINDOWS):
        band = jnp.where((c >= r + halo - w // 2) & (c < r + halo - w // 2 + w), 1.0, 0.0).astype(BF16)
        for s in range(tm // sub):
            hs = hbuf[s * sub:s * sub + sub + 2 * halo, gi * gd:(gi + 1) * gd]
            hi = hs.astype(BF16)
            lo = (hs - hi.astype(F32)).astype(BF16)
            total = (_dot(band, jnp.concatenate([hi, fill], axis=0))
                     + _dot(band, jnp.concatenate([lo, fill], axis=0)))
            t = i * tm + s * sub + t_row
            count = jnp.clip(t - w // 2 + w, 0, seq_len) - jnp.clip(t - w // 2, 0, seq_len)
            centre = hbuf[halo + s * sub:halo + (s + 1) * sub, gi * gd:(gi + 1) * gd]
            diff = total * (1.0 / count.astype(F32)) - centre
            y = _dot(diff.astype(BF16), pw_ref[0, gi]) + pb_ref[0, :, gi * gd:(gi + 1) * gd]
            ybuf[s * sub:(s + 1) * sub, gi * gd:(gi + 1) * gd] = y * ps_ref[0, :, gi * gd:(gi + 1) * gd]

    xo = x + mod(2) * _rmsnorm(ybuf[...], g_ref[0, 1:2, :])
    xo_ref[0] = xo
    h2 = _norm_mod(xo, g_ref[0, 2:3, :], mod(3), mod(4))
    for s in range(SUBLANES):
        h2_ref[pl.ds(s, tm, stride=SUBLANES), :] = h2[:, s * LANES:(s + 1) * LANES]

    logits = _dot(h2.astype(BF16), rw_ref[0]) + rb_ref[0]
    lane = lax.broadcasted_iota(jnp.int32, logits.shape, 1)
    m1 = jnp.max(logits, axis=-1, keepdims=True)
    i1 = jnp.min(jnp.where(logits == m1, lane, LANES), axis=-1, keepdims=True)
    rest = jnp.where(lane == i1, NEG_INF * 2, logits)
    m2 = jnp.max(rest, axis=-1, keepdims=True)
    i2 = jnp.min(jnp.where(rest == m2, lane, LANES), axis=-1, keepdims=True)
    e = jnp.exp(m2 - m1)
    g1 = 1.0 / (1.0 + e)
    g2 = e / (1.0 + e)
    route_ref[0] = jnp.where(lane == 0, i1.astype(F32),
                             jnp.where(lane == 1, i2.astype(F32),
                                       jnp.where(lane == 2, g1, jnp.where(lane == 3, g2, 0.0))))


def _pool_mixer(x, norm_g, mods, layer, mod_row, pool_w, pool_b, pool_scale, router_w, router_b, slot):
    b, l, d = x.shape
    assert d == SUBLANES * LANES
    tm = min(512, l)
    hb = tm // POOL_HALO
    n_halo_blocks = l // POOL_HALO
    n_blk = l // tm
    row = lambda bi, i: (bi, i, 0)
    return pl.pallas_call(
        functools.partial(_pool_kernel, tm=tm, seq_len=l),
        grid=(b, n_blk),
        in_specs=[
            pl.BlockSpec((1, tm, d), row),
            pl.BlockSpec((1, POOL_HALO, d), lambda bi, i: (bi, jnp.maximum(i * hb - 1, 0), 0)),
            pl.BlockSpec((1, POOL_HALO, d), lambda bi, i: (bi, jnp.minimum((i + 1) * hb, n_halo_blocks - 1), 0)),
            _norm_spec(norm_g, layer, 2),
            _mod_spec(mods, layer, lambda bi, i: mod_row(bi)),
            _slot_spec(pool_w, slot),
            _slot_spec(pool_b, slot),
            _slot_spec(pool_scale, slot),
            _slot_spec(router_w, slot),
            _slot_spec(router_b, slot),
        ],
        out_specs=[pl.BlockSpec((1, tm, d), row),
                   pl.BlockSpec((tm * SUBLANES, LANES), lambda bi, i: (bi * n_blk + i, 0)),
                   pl.BlockSpec((1, tm, LANES), row)],
        out_shape=[jax.ShapeDtypeStruct((b, l, d), F32),
                   jax.ShapeDtypeStruct((b * l * SUBLANES, LANES), F32),
                   jax.ShapeDtypeStruct((b, l, LANES), F32)],
        scratch_shapes=[pltpu.VMEM((tm + 2 * POOL_HALO, d), F32), pltpu.VMEM((tm, d), F32)],
        compiler_params=_cparams(2),
        name="pool_mixer",
    )(x, x, x, norm_g, mods, pool_w, pool_b, pool_scale, router_w, router_b)


def _route_plan(route, tm):
    m = route.shape[0]
    n_tiles = (TOP_K * m) // tm + N_EXPERTS
    expert = route[:, :TOP_K].astype(jnp.int32).reshape(-1)
    onehot = (expert[:, None] == jnp.arange(N_EXPERTS, dtype=jnp.int32)[None, :]).astype(jnp.int32)
    csum = jnp.cumsum(onehot, axis=0)
    counts = csum[-1]
    rank = jnp.sum(csum * onehot, axis=1) - 1
    padded = ((counts + tm - 1) // tm) * tm
    ends = jnp.cumsum(padded)
    starts = ends - padded
    pos = jnp.sum(starts[None, :] * onehot, axis=1) + rank
    tile_start = jnp.arange(n_tiles, dtype=jnp.int32) * tm
    tile_expert = jnp.minimum(jnp.sum((tile_start[:, None] >= ends[None, :]).astype(jnp.int32), axis=1),
                              N_EXPERTS - 1)
    n_used = (ends[-1] // tm).reshape(1)
    i32 = lambda a: a.astype(jnp.int32)
    return i32(pos), i32(tile_expert), i32(n_used), i32(jnp.concatenate([starts, ends, n_used]))


def _tile(ref, row):
    return ref.at[pl.ds(pl.multiple_of(row * SUBLANES, SUBLANES), SUBLANES)]


def _wait_tiles(n, src_ref, dst_ref, sem):
    def wait(r, carry):
        pltpu.make_async_copy(_tile(src_ref, 0), _tile(dst_ref, 0), sem).wait()
        return carry
    lax.fori_loop(0, n, wait, 0, unroll=8)


def _dispatch_kernel(pos_ref, seg_ref, h_ref, xs_hbm, zeros_ref, sem, *, tc, tm):
    i = pl.program_id(0)
    zero_rows = zeros_ref.shape[0] // SUBLANES
    n_tiles = xs_hbm.shape[0] // (tm * SUBLANES)

    def clear_tile(first_row):
        for c in range(tm // zero_rows):
            first = pl.multiple_of((first_row + c * zero_rows) * SUBLANES, SUBLANES)
            cp = pltpu.make_async_copy(zeros_ref, xs_hbm.at[pl.ds(first, zero_rows * SUBLANES)], sem)
            cp.start()
            cp.wait()

    @pl.when(i == 0)
    def _():
        zeros_ref[...] = jnp.zeros_like(zeros_ref)
        for e in range(N_EXPERTS):
            start, end = seg_ref[e], seg_ref[N_EXPERTS + e]
            pl.when(end > start)(functools.partial(clear_tile, end - tm))
            spare = seg_ref[2 * N_EXPERTS] + e
            pl.when(spare < n_tiles)(functools.partial(clear_tile, spare * tm))

    base = i * tc * TOP_K

    def issue(r, carry):
        for k in range(TOP_K):
            p = pos_ref[base + TOP_K * r + k]
            pltpu.make_async_copy(_tile(h_ref, r), _tile(xs_hbm, p), sem).start(priority=k)
        return carry
    lax.fori_loop(0, tc, issue, 0, unroll=8)
    _wait_tiles(TOP_K * tc, h_ref, xs_hbm, sem)


def _dispatch(h2t, pos, seg, n_rows, tm):
    m = h2t.shape[0] // SUBLANES
    tc = min(1024, m)
    return pl.pallas_call(
        functools.partial(_dispatch_kernel, tc=tc, tm=tm),
        grid_spec=pltpu.PrefetchScalarGridSpec(
            num_scalar_prefetch=2,
            grid=(m // tc,),
            in_specs=[pl.BlockSpec((tc * SUBLANES, LANES), lambda i, pos, seg: (i, 0))],
            out_specs=pl.BlockSpec(memory_space=pl.ANY),
            scratch_shapes=[pltpu.VMEM((min(tm, 256) * SUBLANES, LANES), F32), pltpu.SemaphoreType.DMA],
        ),
        out_shape=jax.ShapeDtypeStruct((n_rows * SUBLANES, LANES), F32),
        compiler_params=_cparams(1),
        name="moe_dispatch",
    )(pos, seg, h2t)


def _experts_kernel(te_ref, nused_ref, xs_ref, wgu_ref, wd_ref, ys_ref, *, tm, n_chunks):
    i = pl.program_id(0)
    d_ff = wd_ref.shape[1]
    tf = d_ff // n_chunks

    @pl.when(i < nused_ref[0])
    def _():
        x = jnp.concatenate([xs_ref[pl.ds(s, tm, stride=SUBLANES), :].astype(BF16)
                             for s in range(SUBLANES)], axis=1)
        acc = None
        for c in range(n_chunks):
            gate = _dot(x, wgu_ref[0, :, c * tf:(c + 1) * tf])
            up = _dot(x, wgu_ref[0, :, d_ff + c * tf:d_ff + (c + 1) * tf])
            part = _dot((_silu(gate) * up).astype(BF16), wd_ref[0, c * tf:(c + 1) * tf, :])
            acc = part if acc is None else acc + part
        for s in range(SUBLANES):
            ys_ref[pl.ds(s, tm, stride=SUBLANES), :] = acc[:, s * LANES:(s + 1) * LANES]

    @pl.when(i >= nused_ref[0])
    def _():
        ys_ref[...] = jnp.zeros_like(ys_ref)


def _experts(xs, tile_expert, n_used, w_gu, w_down, tm):
    n_rows = xs.shape[0] // SUBLANES
    resident = pl.Buffered(1)
    return pl.pallas_call(
        functools.partial(_experts_kernel, tm=tm, n_chunks=2),
        grid_spec=pltpu.PrefetchScalarGridSpec(
            num_scalar_prefetch=2,
            grid=(n_rows // tm,),
            in_specs=[
                pl.BlockSpec((tm * SUBLANES, LANES), lambda i, te, nu: (i, 0)),
                pl.BlockSpec((1,) + w_gu.shape[1:], lambda i, te, nu: (te[i], 0, 0), pipeline_mode=resident),
                pl.BlockSpec((1,) + w_down.shape[1:], lambda i, te, nu: (te[i], 0, 0)),
            ],
            out_specs=pl.BlockSpec((tm * SUBLANES, LANES), lambda i, te, nu: (i, 0)),
        ),
        out_shape=jax.ShapeDtypeStruct((n_rows * SUBLANES, LANES), F32),
        compiler_params=_cparams(1),
        name="moe_experts",
    )(tile_expert, n_used, xs, w_gu, w_down)


def _combine_kernel(pos_ref, ys_hbm, route_ref, x_ref, g_ref, mod_ref, o_ref, buf, sem, *, tc):
    i = pl.program_id(0)
    n = pl.num_programs(0)

    def gather(block, slot):
        base = block * tc * TOP_K

        def issue(r, carry):
            for k in range(TOP_K):
                p = pos_ref[base + TOP_K * r + k]
                pltpu.make_async_copy(_tile(ys_hbm, p), _tile(buf.at[slot, k], r), sem.at[slot]).start(priority=k)
            return carry
        lax.fori_loop(0, tc, issue, 0, unroll=8)

    @pl.when(i == 0)
    def _():
        gather(0, 0)

    @pl.when(i + 1 < n)
    def _():
        gather(i + 1, (i + 1) % 2)

    slot = i % 2
    _wait_tiles(TOP_K * tc, ys_hbm, buf.at[slot, 0], sem.at[slot])
    rows = lambda k: jnp.concatenate([buf[slot, k, pl.ds(s, tc, stride=SUBLANES), :] for s in range(SUBLANES)], axis=1)
    route = route_ref[...]
    f = route[:, 2:3] * rows(0) + route[:, 3:4] * rows(1)
    o_ref[...] = x_ref[...] + mod_ref[0, 0, 5:6, :] * _rmsnorm(f, g_ref[0, 3:4, :])


def _combine(ys, pos, route, x, norm_g, mods, layer, mod_row_of_block, tc):
    m, d = x.shape
    return pl.pallas_call(
        functools.partial(_combine_kernel, tc=tc),
        grid_spec=pltpu.PrefetchScalarGridSpec(
            num_scalar_prefetch=1,
            grid=(m // tc,),
            in_specs=[
                pl.BlockSpec(memory_space=pl.ANY),
                pl.BlockSpec((tc, LANES), lambda i, pos: (i, 0)),
                pl.BlockSpec((tc, d), lambda i, pos: (i, 0)),
                _norm_spec(norm_g, layer, 1),
                _mod_spec(mods, layer, lambda i, pos: mod_row_of_block(i)),
            ],
            out_specs=pl.BlockSpec((tc, d), lambda i, pos: (i, 0)),
            scratch_shapes=[pltpu.VMEM((2, TOP_K, tc * SUBLANES, LANES), F32),
                            pltpu.SemaphoreType.DMA((2,))],
        ),
        out_shape=jax.ShapeDtypeStruct((m, d), F32),
        compiler_params=_cparams(1),
        name="moe_combine",
    )(pos, ys, route, x, norm_g, mods)


def _moe(h2t, route, x, norm_g, mods, layer, mod_row, w_gu, w_down):
    b, l, d = x.shape
    m = b * l
    tm = min(512, l)
    tc = min(256, l)
    route2 = route.reshape(m, LANES)
    pos, tile_expert, n_used, seg = _route_plan(route2, tm)
    n_rows = TOP_K * m + N_EXPERTS * tm
    xs = _dispatch(h2t, pos, seg, n_rows, tm)
    ys = _experts(xs, tile_expert, n_used, w_gu, w_down, tm)
    blocks_per_seq = l // tc
    out = _combine(ys, pos, route2, x.reshape(m, d), norm_g, mods, layer,
                   lambda i: mod_row(i // blocks_per_seq), tc)
    return out.reshape(b, l, d)


def kernel(x, c, ctx, c_ctx, ada_w, ada_b, norm_g, attn_w_qkv, attn_w_o, attn_sink, pool_w, pool_b,
           pool_scale, ffn_w_gu, ffn_w_down, router_w, router_b, moe_w_gu, moe_w_down):
    batch, seq, d = x.shape
    depth = ada_w.shape[0]
    ctx_row = batch
    assert batch < MOD_ROWS and seq % ATTN_BLOCK == 0 and d == N_HEADS * HEAD_DIM

    cond = jnp.zeros((MOD_ROWS, d), F32).at[:batch].set(c).at[ctx_row].set(c_ctx)
    mods = _ada_mods(cond, ada_w, ada_b).reshape(depth, MOD_ROWS, 6, d)
    latent_row = lambda bi: bi
    context_row = lambda bi: ctx_row
    rope = _rope_tables(seq)

    w_qkv = attn_w_qkv.astype(BF16)
    w_o = attn_w_o.astype(BF16)
    w_gu = ffn_w_gu.astype(BF16)
    w_down = ffn_w_down.astype(BF16)
    p_w = pool_w.astype(BF16)
    p_b = pool_b.reshape(-1, 1, d)
    p_s = pool_scale.reshape(-1, 1, d)
    n_moe, n_exp, _, gu_cols = moe_w_gu.shape
    d_ff_e = moe_w_down.shape[2]
    gu_src = moe_w_gu.reshape(n_moe, n_exp * d, gu_cols)
    down_src = moe_w_down.reshape(n_moe, n_exp * d_ff_e, d)
    experts_bf16 = {}
    r_w = jnp.zeros((router_w.shape[0], d, LANES), BF16).at[:, :, :n_exp].set(router_w.astype(BF16))
    r_b = jnp.full((router_b.shape[0], 1, LANES), NEG_INF, F32).at[:, 0, :n_exp].set(router_b)

    is_attn = [i % 2 == 0 for i in range(depth)]
    for i in range(depth):
        slot = i // 2
        ctx_later = any(is_attn[i + 1:])
        if is_attn[i]:
            q, kv = _qkv_proj(x, norm_g, mods, i, latent_row, w_qkv, slot, rope)
            qc, kvc = _qkv_proj(ctx, norm_g, mods, i, context_row, w_qkv, slot, None)
            if i + 1 < depth and not is_attn[i + 1]:
                moe_slot = (i + 1) // 2
                attn, gu = _attention(q, kv, kvc, attn_sink, slot, True, cast=(gu_src, moe_slot))
                x, down = _oproj_ffn(attn, x, norm_g, mods, i, latent_row, w_o, w_gu, w_down, slot,
                                     cast=(down_src, moe_slot))
                experts_bf16[moe_slot] = (gu.reshape(n_exp, d, gu_cols), down.reshape(n_exp, d_ff_e, d))
            else:
                attn = _attention(q, kv, kvc, attn_sink, slot, True)
                x = _oproj_ffn(attn, x, norm_g, mods, i, latent_row, w_o, w_gu, w_down, slot)
            if ctx_later:
                attn_c = _attention(qc, None, kvc, attn_sink, slot, False)
                ctx = _oproj_ffn(attn_c, ctx, norm_g, mods, i, context_row, w_o, w_gu, w_down, slot)
        else:
            if slot not in experts_bf16:
                experts_bf16[slot] = (moe_w_gu[slot].astype(BF16), moe_w_down[slot].astype(BF16))
            e_gu, e_down = experts_bf16[slot]
            x1, h2t, route = _pool_mixer(x, norm_g, mods, i, latent_row, p_w, p_b, p_s, r_w, r_b, slot)
            x = _moe(h2t, route, x1, norm_g, mods, i, latent_row, e_gu, e_down)
            if ctx_later:
                c1, h2c, route_c = _pool_mixer(ctx, norm_g, mods, i, context_row, p_w, p_b, p_s, r_w, r_b, slot)
                ctx = _moe(h2c, route_c, c1, norm_g, mods, i, context_row, e_gu, e_down)
    return x
```
